```python
import math, functools
import jax, jax.numpy as jnp
from jax import lax
import numpy as np

D_MODEL = 2048
BATCH = 2
SEQ = 4096
DEPTH = 1
DEC_BATCH = 8
DEC_SEQ = 4
PAST_LEN = 16384
PAGE_SIZE = 128

POOL_WINDOWS = (2, 4, 8, 16)
POOL_GROUPS = len(POOL_WINDOWS)
POOL_WIDTH = D_MODEL // 2
POOL_GROUP_WIDTH = POOL_WIDTH // POOL_GROUPS
POOL_HIST = max(POOL_WINDOWS) - 1

ATTN_GROUPS = ((128, 1), (512, 4), (2048, 16))
N_ATTN_GROUPS = len(ATTN_GROUPS)
HEADS_PER_GROUP = 8
HEAD_DIM = 128
ATTN_WIDTH = HEADS_PER_GROUP * HEAD_DIM

QKV_OFF = POOL_WIDTH
GATE_OFF = QKV_OFF + N_ATTN_GROUPS * 3 * ATTN_WIDTH
PROJ_WIDTH = GATE_OFF + 2 * D_MODEL

PEER_HEADS = 8
PEER_NKEYS = 128
PEER_EXPERTS = PEER_NKEYS * PEER_NKEYS
PEER_QDIM = 256
PEER_TOPK = 16
PEER_CHUNK = 128

NORM_EPS = 1e-6
NEG_INF = -1e30

kernel_name = 'hybrid_pool_dilated_peer_adaln_step'


def rms_norm(x, gain):
    x32 = x.astype(jnp.float32)
    y = x32 * lax.rsqrt(jnp.mean(x32 * x32, axis=-1, keepdims=True) + NORM_EPS)
    return (y * gain.astype(jnp.float32)).astype(x.dtype)


def adaln_mod(c, w_ada, b_ada):
    mod = jax.nn.silu(c) @ w_ada + b_ada
    return jnp.split(mod[:, None, :], 6, axis=-1)


def alibi_slopes(group):
    idx = jnp.arange(HEADS_PER_GROUP, dtype=jnp.float32) + (group * HEADS_PER_GROUP + 1)
    return jnp.exp2(-8.0 * idx / (N_ATTN_GROUPS * HEADS_PER_GROUP))


def pool_mix(u, hist, start, pool_w, pool_scale):
    N, T, P = u.shape
    ext = jnp.concatenate([hist.astype(u.dtype), u], axis=1).astype(jnp.float32)
    cs = jnp.pad(jnp.cumsum(ext, axis=1), ((0, 0), (1, 0), (0, 0)))
    end = cs[:, POOL_HIST + 1:POOL_HIST + 1 + T]
    pos = start + jnp.arange(T)
    means = []
    for g, w in enumerate(POOL_WINDOWS):
        sl = slice(g * POOL_GROUP_WIDTH, (g + 1) * POOL_GROUP_WIDTH)
        win_sum = end[..., sl] - cs[:, POOL_HIST + 1 - w:POOL_HIST + 1 - w + T, sl]
        count = jnp.minimum(w, pos + 1).astype(jnp.float32)
        means.append(win_sum / count[None, :, None])
    pooled = jnp.concatenate(means, axis=-1) - ext[:, POOL_HIST:]
    mixed = jnp.einsum('ntgc,gcd->ntgd', pooled.reshape(N, T, POOL_GROUPS, POOL_GROUP_WIDTH),
                       pool_w.astype(jnp.float32))
    return (mixed.reshape(N, T, P) * pool_scale.astype(jnp.float32)).astype(u.dtype)


def dilated_attn_prompt(q, k, v, window, dil, slopes):
    B, T, H, E = q.shape
    n_ctx = window // dil
    blk = n_ctx
    span = dil * blk
    t_pad = -(-T // span) * span
    nb = t_pad // span

    def to_blocks(a):
        a = jnp.pad(a, ((0, 0), (0, t_pad - T), (0, 0), (0, 0))).reshape(B, nb * blk, dil, H, E)
        return a.transpose(0, 2, 1, 3, 4).reshape(B, dil, nb, blk, H, E)

    def with_prev(a):
        prev = jnp.pad(a, ((0, 0), (0, 0), (1, 0), (0, 0), (0, 0), (0, 0)))[:, :, :-1]
        return jnp.concatenate([prev, a], axis=3)

    def from_blocks(a):
        a = a.reshape((B, dil, nb * blk) + a.shape[4:]).swapaxes(1, 2)
        return a.reshape((B, t_pad) + a.shape[3:])[:, :T]

    qb, kb, vb = to_blocks(q), to_blocks(k), to_blocks(v)
    k2, v2 = with_prev(kb), with_prev(vb)
    i = jnp.arange(blk)[:, None]
    j = jnp.arange(2 * blk)[None, :]
    delta = blk + i - j
    nblk = jnp.arange(nb)[:, None, None]
    valid = (delta >= 0) & (delta <= n_ctx) & (nblk * blk - blk + j >= 0)
    s = jnp.einsum('brnqhe,brnkhe->brnhqk', qb, k2) / math.sqrt(E)
    s = s - slopes[:, None, None] * (delta * dil).astype(jnp.float32)
    s = jnp.where(valid[:, None], s, NEG_INF)
    m = jnp.max(s, axis=-1)
    p = jnp.exp(s - m[..., None])
    l = jnp.swapaxes(jnp.sum(p, axis=-1), 3, 4)
    o = jnp.einsum('brnhqk,brnkhe->brnqhe', p, v2) / l[..., None]
    return from_blocks(o), from_blocks(jnp.swapaxes(m, 3, 4)), from_blocks(l)


def dilated_attn_step(q, k, v, buf, window, dil, slopes):
    N, S, H, E = q.shape
    L = buf.shape[1]
    n_ctx = window // dil
    ext_k = jnp.concatenate([buf[:, :, 0], k.astype(buf.dtype)], axis=1)
    ext_v = jnp.concatenate([buf[:, :, 1], v.astype(buf.dtype)], axis=1)
    steps = jnp.arange(n_ctx + 1)
    idx = L + jnp.arange(S)[:, None] - steps[None, :] * dil
    valid = idx >= 0
    idx = jnp.maximum(idx, 0)
    kg = ext_k.astype(jnp.float32)[:, idx]
    vg = ext_v.astype(jnp.float32)[:, idx]
    s = jnp.einsum('nshe,nskhe->nshk', q.astype(jnp.float32), kg) / math.sqrt(E)
    s = s - slopes[:, None] * (steps * dil).astype(jnp.float32)
    s = jnp.where(valid[:, None, :], s, NEG_INF)
    m = jnp.max(s, axis=-1)
    p = jnp.exp(s - m[..., None])
    l = jnp.sum(p, axis=-1)
    o = jnp.einsum('nshk,nskhe->nshe', p, vg) / l[..., None]
    new_buf = jnp.stack([ext_k, ext_v], axis=2)[:, -min(window, L + S):]
    return (o, m, l), new_buf


def combine_dilations(outs):
    m_all = functools.reduce(jnp.maximum, [m for _, m, _ in outs])
    ws = [l * jnp.exp(m - m_all) for _, m, l in outs]
    num = sum(w[..., None] * o for (o, _, _), w in zip(outs, ws))
    return num / sum(ws)[..., None]


def attend_prompt(qkv):
    T = qkv.shape[1]
    outs, new_bufs = [], []
    for g, (win, dil) in enumerate(ATTN_GROUPS):
        q, k, v = qkv[:, :, g, 0], qkv[:, :, g, 1], qkv[:, :, g, 2]
        outs.append(dilated_attn_prompt(q.astype(jnp.float32), k.astype(jnp.float32),
                                        v.astype(jnp.float32), win, dil, alibi_slopes(g)))
        new_bufs.append(jnp.stack([k, v], axis=2)[:, -min(win, T):])
    return combine_dilations(outs).astype(qkv.dtype), new_bufs


def attend_sample(qkv, bufs):
    outs, new_bufs = [], []
    for g, (win, dil) in enumerate(ATTN_GROUPS):
        q, k, v = qkv[:, :, g, 0], qkv[:, :, g, 1], qkv[:, :, g, 2]
        res, nbuf = dilated_attn_step(q, k, v, bufs[g], win, dil, alibi_slopes(g))
        outs.append(res)
        new_bufs.append(nbuf)
    return combine_dilations(outs).astype(qkv.dtype), new_bufs


def peer_ffn(xn, w_pq, sub_keys, peer_u, peer_v):
    N, T, D = xn.shape
    n_tok = N * T
    xf = xn.reshape(n_tok, D)
    q = (xf @ w_pq).astype(jnp.float32).reshape(n_tok, PEER_HEADS, 2, PEER_QDIM // 2)
    sc = jnp.einsum('thpc,hpkc->thpk', q, sub_keys.astype(jnp.float32))
    vals, idx = lax.top_k(sc, PEER_TOPK)
    cand = vals[:, :, 0, :, None] + vals[:, :, 1, None, :]
    cand_id = idx[:, :, 0, :, None] * PEER_NKEYS + idx[:, :, 1, None, :]
    best, pos = lax.top_k(cand.reshape(n_tok, PEER_HEADS, -1), PEER_TOPK)
    expert = jnp.take_along_axis(cand_id.reshape(n_tok, PEER_HEADS, -1), pos, axis=-1)
    gate = jax.nn.softmax(best, axis=-1)
    hk = PEER_HEADS * PEER_TOPK
    ch = min(PEER_CHUNK, n_tok)
    n_pad = -(-n_tok // ch) * ch
    xp = jnp.pad(xf, ((0, n_pad - n_tok), (0, 0))).reshape(-1, ch, D)
    ep = jnp.pad(expert.reshape(n_tok, hk), ((0, n_pad - n_tok), (0, 0))).reshape(-1, ch, hk)
    gp = jnp.pad(gate.reshape(n_tok, hk), ((0, n_pad - n_tok), (0, 0))).reshape(-1, ch, hk)

    def one_chunk(args):
        xc, ec, gc = args
        u = jnp.take(peer_u, ec, axis=0)
        act = jnp.einsum('cd,ced->ce', xc, u).astype(jnp.float32)
        act = (jax.nn.gelu(act) * gc).astype(xc.dtype)
        return jnp.einsum('ce,ced->cd', act, jnp.take(peer_v, ec, axis=0))

    out = lax.map(one_chunk, (xp, ep, gp))
    return out.reshape(n_pad, D)[:n_tok].reshape(N, T, D)


def trunk_layer(x, c, pool_hist, pool_start, attend, w_ada, b_ada, g_norm1, g_norm2, w_in, pool_w, pool_scale,
                w_pool_proj, w_attn_proj, w_out, w_pq, sub_keys, peer_u, peer_v):
    N, T, _ = x.shape
    sh1, sc1, ga1, sh2, sc2, ga2 = adaln_mod(c, w_ada, b_ada)
    n1 = rms_norm(x, g_norm1) * (1 + sc1) + sh1
    z = n1 @ w_in
    u = z[..., :QKV_OFF]
    qkv = z[..., QKV_OFF:GATE_OFF].reshape(N, T, N_ATTN_GROUPS, 3, HEADS_PER_GROUP, HEAD_DIM)
    gate_pool = jax.nn.sigmoid(z[..., GATE_OFF:GATE_OFF + D_MODEL])
    gate_attn = jax.nn.sigmoid(z[..., GATE_OFF + D_MODEL:])
    pool_out = pool_mix(u, pool_hist, pool_start, pool_w, pool_scale)
    attn_out, new_kv = attend(qkv)
    merged = (gate_pool * (pool_out @ w_pool_proj)
              + gate_attn * (attn_out.reshape(N, T, ATTN_WIDTH) @ w_attn_proj))
    h = x + ga1 * (merged @ w_out)
    n2 = rms_norm(h, g_norm2) * (1 + sc2) + sh2
    h = h + ga2 * peer_ffn(n2, w_pq, sub_keys, peer_u, peer_v)
    new_pool = jnp.concatenate([pool_hist.astype(u.dtype), u], axis=1)[:, -POOL_HIST:]
    return h, new_pool, new_kv


def setup_inputs(seed: int = 0) -> dict:
    key = jax.random.key(seed)
    ks = jax.random.split(key, 26)
    f32 = jnp.float32
    D = D_MODEL

    def nrm(k, shape, scale):
        return jax.random.normal(k, shape, f32) * scale

    inputs = {
        'x_prompt': nrm(ks[0], (BATCH, SEQ, D), 1.0),
        'x_sample': nrm(ks[1], (DEC_BATCH, DEC_SEQ, D), 1.0),
        'state_pool': nrm(ks[2], (DEPTH, DEC_BATCH, POOL_HIST, POOL_WIDTH), 1.0),
        'cache_kv0': nrm(ks[3], (DEPTH, DEC_BATCH, min(ATTN_GROUPS[0][0], PAST_LEN), 2, HEADS_PER_GROUP, HEAD_DIM), 1.0),
        'cache_kv1': nrm(ks[4], (DEPTH, DEC_BATCH, min(ATTN_GROUPS[1][0], PAST_LEN), 2, HEADS_PER_GROUP, HEAD_DIM), 1.0),
        'cache_kv2': nrm(ks[5], (DEPTH, DEC_BATCH, min(ATTN_GROUPS[2][0], PAST_LEN), 2, HEADS_PER_GROUP, HEAD_DIM), 1.0),
        'c_prompt': nrm(ks[6], (BATCH, D), 1.0),
        'c_sample': nrm(ks[7], (DEC_BATCH, D), 1.0),
        'w_ada': nrm(ks[8], (DEPTH, D, 6 * D), 0.5 * D ** -0.5),
        'b_ada': nrm(ks[9], (DEPTH, 6 * D), 0.02),
        'g_norm1': 1.0 + nrm(ks[10], (DEPTH, D), 0.05),
        'g_norm2': 1.0 + nrm(ks[11], (DEPTH, D), 0.05),
        'w_in': nrm(ks[12], (DEPTH, D, PROJ_WIDTH), D ** -0.5),
        'pool_w': nrm(ks[13], (DEPTH, POOL_GROUPS, POOL_GROUP_WIDTH, POOL_GROUP_WIDTH), POOL_GROUP_WIDTH ** -0.5),
        'pool_scale': 1.0 + nrm(ks[14], (DEPTH, POOL_WIDTH), 0.05),
        'w_pool_proj': nrm(ks[15], (DEPTH, POOL_WIDTH, D), POOL_WIDTH ** -0.5),
        'w_attn_proj': nrm(ks[16], (DEPTH, ATTN_WIDTH, D), ATTN_WIDTH ** -0.5),
        'w_out': nrm(ks[17], (DEPTH, D, D), D ** -0.5),
        'w_pq': nrm(ks[18], (DEPTH, D, PEER_HEADS * PEER_QDIM), D ** -0.5),
        'sub_keys': nrm(ks[19], (DEPTH, PEER_HEADS, 2, PEER_NKEYS, PEER_QDIM // 2), (PEER_QDIM // 2) ** -0.5),
        'peer_u': nrm(ks[20], (DEPTH, PEER_EXPERTS, D), D ** -0.5),
        'peer_v': nrm(ks[21], (DEPTH, PEER_EXPERTS, D), PEER_HEADS ** -0.5),
        'g_final': 1.0 + nrm(ks[22], (D,), 0.05),
    }
    return inputs


def reference(x_prompt, x_sample, state_pool, cache_kv0, cache_kv1, cache_kv2, c_prompt, c_sample,
              w_ada, b_ada, g_norm1, g_norm2, w_in, pool_w, pool_scale, w_pool_proj, w_attn_proj, w_out,
              w_pq, sub_keys, peer_u, peer_v, g_final):
    caches = (cache_kv0, cache_kv1, cache_kv2)
    hp, hs = x_prompt, x_sample
    pool_p, pool_s = [], []
    kv_p = [[] for _ in ATTN_GROUPS]
    kv_s = [[] for _ in ATTN_GROUPS]
    for layer in range(DEPTH):
        weights = (w_ada[layer], b_ada[layer], g_norm1[layer], g_norm2[layer], w_in[layer], pool_w[layer],
                   pool_scale[layer], w_pool_proj[layer], w_attn_proj[layer], w_out[layer], w_pq[layer],
                   sub_keys[layer], peer_u[layer], peer_v[layer])
        hist0 = jnp.zeros((hp.shape[0], POOL_HIST, POOL_WIDTH), hp.dtype)
        hp, new_pool, new_kv = trunk_layer(hp, c_prompt, hist0, 0, attend_prompt, *weights)
        pool_p.append(new_pool)
        for g in range(N_ATTN_GROUPS):
            kv_p[g].append(new_kv[g])
        attend_s = functools.partial(attend_sample, bufs=tuple(cache[layer] for cache in caches))
        hs, new_pool, new_kv = trunk_layer(hs, c_sample, state_pool[layer], PAST_LEN, attend_s, *weights)
        pool_s.append(new_pool)
        for g in range(N_ATTN_GROUPS):
            kv_s[g].append(new_kv[g])
    y_prompt = rms_norm(hp, g_final)
    y_sample = rms_norm(hs, g_final)
    return (y_prompt, y_sample, jnp.stack(pool_p), jnp.stack(kv_p[0]), jnp.stack(kv_p[1]), jnp.stack(kv_p[2]),
            jnp.stack(pool_s), jnp.stack(kv_s[0]), jnp.stack(kv_s[1]), jnp.stack(kv_s[2]))
```

```python
import functools
import math

import jax
import jax.numpy as jnp
from jax import lax
from jax.experimental import pallas as pl
from jax.experimental.pallas import tpu as pltpu

F32 = jnp.float32
BF16 = jnp.bfloat16

D_MODEL = 2048
POOL_WINDOWS = (2, 4, 8, 16)
POOL_GROUPS = len(POOL_WINDOWS)
POOL_WIDTH = D_MODEL // 2
POOL_GROUP_WIDTH = POOL_WIDTH // POOL_GROUPS
POOL_HIST = max(POOL_WINDOWS) - 1
ATTN_GROUPS = ((128, 1), (512, 4), (2048, 16))
N_ATTN_GROUPS = len(ATTN_GROUPS)
HEADS = 8
HEAD_DIM = 128
ATTN_WIDTH = HEADS * HEAD_DIM
ATTN_CTX = 128
QKV_OFF = POOL_WIDTH
GATE_OFF = QKV_OFF + N_ATTN_GROUPS * 3 * ATTN_WIDTH
PROJ_WIDTH = GATE_OFF + 2 * D_MODEL
PEER_HEADS = 8
PEER_NKEYS = 128
PEER_TOPK = 16
NORM_EPS = 1e-6
NEG_INF = -1e30
PAST_LEN = 16384

LANES = 128
HALO = 16
VMEM_LIMIT = 56 * 1024 * 1024
ATTN_TILE = 2048
EXPERT_BLOCK_I = 8


def _params(*sem):
    return pltpu.CompilerParams(dimension_semantics=sem, vmem_limit_bytes=VMEM_LIMIT)


def _dot(a, b):
    return jnp.dot(a, b, preferred_element_type=F32)


def _dot_nt(a, b):
    return lax.dot_general(a, b, (((1,), (1,)), ((), ())), preferred_element_type=F32)


def _ada_body(c_ref, w_ref, b_ref, o_ref):
    c = c_ref[...]
    s = c * jax.nn.sigmoid(c)
    o_ref[...] = _dot(s.astype(BF16), w_ref[...].astype(BF16)) + b_ref[...]


def _adaln(c_all, w_ada, b_ada):
    rows, d = c_all.shape
    n = w_ada.shape[1]
    tn = 1024
    return pl.pallas_call(
        _ada_body, grid=(n // tn,),
        in_specs=[pl.BlockSpec((rows, d), lambda j: (0, 0)),
                  pl.BlockSpec((d, tn), lambda j: (0, j)),
                  pl.BlockSpec((1, tn), lambda j: (0, j))],
        out_specs=pl.BlockSpec((rows, tn), lambda j: (0, j)),
        out_shape=jax.ShapeDtypeStruct((rows, n), F32),
        compiler_params=_params("parallel"), name="adaln",
    )(c_all, w_ada, b_ada.reshape(1, n))


class _Mod:
    def __init__(self, arr, rows_per_group):
        self.arr = arr
        self.rows_per_group = rows_per_group

    def spec(self, chunk, tm, tn=D_MODEL, ncols=1):
        per = D_MODEL // tn
        if self.rows_per_group is None:
            if ncols == 1:
                return pl.BlockSpec((tm, tn), lambda i: (i, chunk * per))
            return pl.BlockSpec((tm, tn), lambda i, j: (i, chunk * per + j))
        tpg = self.rows_per_group // tm
        if ncols == 1:
            return pl.BlockSpec((None, 1, tn), lambda i: (i // tpg, 0, chunk * per))
        return pl.BlockSpec((None, 1, tn), lambda i, j: (i // tpg, 0, chunk * per + j))


def _norm_mod_body(x_ref, g_ref, sc_ref, sh_ref, o_ref):
    x = x_ref[...]
    y = x * lax.rsqrt(jnp.mean(x * x, axis=-1, keepdims=True) + NORM_EPS) * g_ref[...]
    o_ref[...] = (y * (1.0 + sc_ref[...]) + sh_ref[...]).astype(o_ref.dtype)


def _norm_mod(x, gain, mod, shift_chunk, scale_chunk, tm):
    rows, d = x.shape
    return pl.pallas_call(
        _norm_mod_body, grid=(rows // tm,),
        in_specs=[pl.BlockSpec((tm, d), lambda i: (i, 0)),
                  pl.BlockSpec((1, d), lambda i: (0, 0)),
                  mod.spec(scale_chunk, tm), mod.spec(shift_chunk, tm)],
        out_specs=pl.BlockSpec((tm, d), lambda i: (i, 0)),
        out_shape=jax.ShapeDtypeStruct((rows, d), BF16),
        compiler_params=_params("parallel"), name="norm_mod",
    )(x, gain.reshape(1, d), mod.arr, mod.arr)


def _mm_body(x_ref, w_ref, o_ref):
    o_ref[...] = _dot(x_ref[...], w_ref[...]).astype(o_ref.dtype)


def _matmul(x, w, tm, tn, out_dtype=F32):
    rows, k = x.shape
    n = w.shape[1]
    return pl.pallas_call(
        _mm_body, grid=(rows // tm, n // tn),
        in_specs=[pl.BlockSpec((tm, k), lambda i, j: (i, 0)),
                  pl.BlockSpec((k, tn), lambda i, j: (0, j))],
        out_specs=pl.BlockSpec((tm, tn), lambda i, j: (i, j)),
        out_shape=jax.ShapeDtypeStruct((rows, n), out_dtype),
        compiler_params=_params("parallel", "parallel"), name="in_proj",
    )(x, w)


def _pool_body(u_ref, h_ref, pw_ref, ps_ref, o_ref, ext_ref, *, tm, zero_first, pos0):
    i = pl.program_id(1)
    halo = h_ref[...]
    if zero_first:
        halo = jnp.where(i == 0, 0.0, halo)
    ext_ref[0:HALO, :] = halo
    ext_ref[HALO:HALO + tm, :] = u_ref[...]
    pos = pos0 + i * tm + lax.broadcasted_iota(jnp.int32, (tm, 1), 0)
    for g, w in enumerate(POOL_WINDOWS):
        cols = slice(g * POOL_GROUP_WIDTH, (g + 1) * POOL_GROUP_WIDTH)
        tok = ext_ref[HALO:HALO + tm, cols]
        win = tok
        for back in range(1, w):
            win = win + ext_ref[HALO - back:HALO - back + tm, cols]
        count = jnp.minimum(w, pos + 1).astype(F32)
        pooled = win / count - tok
        mixed = _dot(pooled.astype(BF16), pw_ref[g])
        o_ref[:, cols] = (mixed * ps_ref[:, cols]).astype(o_ref.dtype)


def _pool(u_arr, halo_arr, u_spec, halo_spec, pool_w, pool_scale, grid, tm, zero_first, pos0):
    rows = grid[0] * grid[1] * tm
    body = functools.partial(_pool_body, tm=tm, zero_first=zero_first, pos0=pos0)
    return pl.pallas_call(
        body, grid=grid,
        in_specs=[u_spec, halo_spec,
                  pl.BlockSpec((POOL_GROUPS, POOL_GROUP_WIDTH, POOL_GROUP_WIDTH), lambda b, i: (0, 0, 0)),
                  pl.BlockSpec((1, POOL_WIDTH), lambda b, i: (0, 0))],
        out_specs=pl.BlockSpec((tm, POOL_WIDTH), lambda b, i: (b * grid[1] + i, 0)),
        out_shape=jax.ShapeDtypeStruct((rows, POOL_WIDTH), BF16),
        scratch_shapes=[pltpu.VMEM((HALO + tm, POOL_WIDTH), F32)],
        compiler_params=_params("parallel", "parallel"), name="pool_mix",
    )(u_arr, halo_arr, pool_w, pool_scale.reshape(1, POOL_WIDTH))


def _rows(start, dil):
    return pl.ds(start, ATTN_CTX, stride=dil) if dil > 1 else pl.ds(start, ATTN_CTX)


def _attn_prompt_body(sl_ref, *refs):
    ins, o_ref, scr = refs[:15], refs[15], refs[16:]
    acc_ref, m_ref, l_ref = scr[6:]
    t = pl.program_id(1)
    head = pl.program_id(2)
    steps_cur =(lax.broadcasted_iota(jnp.int32, (ATTN_CTX, ATTN_CTX), 0)
                 - lax.broadcasted_iota(jnp.int32, (ATTN_CTX, ATTN_CTX), 1)).astype(F32)
    steps_prev = steps_cur + float(ATTN_CTX)
    scale = 1.0 / math.sqrt(HEAD_DIM)
    for g, (window, dil) in enumerate(ATTN_GROUPS):
        q_ref, k_ref, v_ref, kp_ref, vp_ref = ins[5 * g:5 * g + 5]
        kx_ref, vx_ref = scr[2 * g], scr[2 * g + 1]
        kx_ref[0:window, :] = kp_ref[...]
        vx_ref[0:window, :] = vp_ref[...]
        kx_ref[window:window + ATTN_TILE, :] = k_ref[...]
        vx_ref[window:window + ATTN_TILE, :] = v_ref[...]
        slope = sl_ref[g, head] * float(dil)
        bias_cur = slope * steps_cur
        bias_prev = slope * steps_prev

        def block(idx, carry, g=g, window=window, dil=dil, q_ref=q_ref, kx_ref=kx_ref, vx_ref=vx_ref,
                  bias_cur=bias_cur, bias_prev=bias_prev):
            n = idx // dil
            r = idx % dil
            start = n * window + r
            q = q_ref[_rows(start, dil), :].astype(BF16)
            kc = kx_ref[_rows(window + start, dil), :].astype(BF16)
            vc = vx_ref[_rows(window + start, dil), :].astype(BF16)
            kp = kx_ref[_rows(start, dil), :].astype(BF16)
            vp = vx_ref[_rows(start, dil), :].astype(BF16)
            prev_limit = jnp.where(jnp.logical_or(t > 0, n > 0), float(ATTN_CTX), -1.0)
            s_cur = jnp.where(steps_cur >= 0.0, _dot_nt(q, kc) * scale - bias_cur, NEG_INF)
            s_prev = jnp.where(steps_prev <= prev_limit, _dot_nt(q, kp) * scale - bias_prev, NEG_INF)
            m = jnp.maximum(jnp.max(s_cur, axis=1, keepdims=True), jnp.max(s_prev, axis=1, keepdims=True))
            p_cur = jnp.exp(s_cur - m)
            p_prev = jnp.exp(s_prev - m)
            l = jnp.sum(p_cur, axis=1, keepdims=True) + jnp.sum(p_prev, axis=1, keepdims=True)
            acc = _dot(p_cur.astype(BF16), vc) + _dot(p_prev.astype(BF16), vp)
            m = jnp.broadcast_to(m, (ATTN_CTX, LANES))
            l = jnp.broadcast_to(l, (ATTN_CTX, LANES))
            rows = _rows(start, dil)
            if g > 0:
                m_old = m_ref[rows, :]
                m_new = jnp.maximum(m_old, m)
                a_old = jnp.exp(m_old - m_new)
                a_new = jnp.exp(m - m_new)
                acc = acc_ref[rows, :] * a_old + acc * a_new
                l = l_ref[rows, :] * a_old + l * a_new
                m = m_new
            acc_ref[rows, :] = acc
            m_ref[rows, :] = m
            l_ref[rows, :] = l
            return carry

        lax.fori_loop(0, ATTN_TILE // ATTN_CTX, block, 0)
    o_ref[...] = (acc_ref[...] / l_ref[...]).astype(o_ref.dtype)


def _attn_prompt(z, slopes, batch, seq):
    tiles = seq // ATTN_TILE
    col0 = QKV_OFF // HEAD_DIM
    in_specs = [pl.BlockSpec(memory_space=pltpu.SMEM)]
    operands = [slopes]
    scratch = []
    for g, (window, _) in enumerate(ATTN_GROUPS):
        per_tile = ATTN_TILE // window
        for part in range(3):
            col = col0 + (g * 3 + part) * HEADS
            in_specs.append(pl.BlockSpec((ATTN_TILE, HEAD_DIM),
                                         lambda b, t, h, col=col: (b * tiles + t, col + h)))
            operands.append(z)
        for part in (1, 2):
            col = col0 + (g * 3 + part) * HEADS
            in_specs.append(pl.BlockSpec(
                (window, HEAD_DIM),
                lambda b, t, h, col=col, per_tile=per_tile:
                (jnp.maximum((b * tiles + t) * per_tile - 1, 0), col + h)))
            operands.append(z)
        scratch += [pltpu.VMEM((window + ATTN_TILE, HEAD_DIM), F32)] * 2
    scratch += [pltpu.VMEM((ATTN_TILE, LANES), F32)] * 3
    return pl.pallas_call(
        _attn_prompt_body, grid=(batch, tiles, HEADS),
        in_specs=in_specs,
        out_specs=pl.BlockSpec((ATTN_TILE, HEAD_DIM), lambda b, t, h: (b * tiles + t, h)),
        out_shape=jax.ShapeDtypeStruct((batch * seq, ATTN_WIDTH), BF16),
        scratch_shapes=scratch,
        compiler_params=_params("parallel", "parallel", "parallel"), name="attn_prompt",
    )(*operands)


def _attn_sample_body(sl_ref, qkv_ref, c0_ref, c1_ref, c2_ref, o_ref, *, dec_seq, pad_rows):
    caches = (c0_ref, c1_ref, c2_ref)
    scale = 1.0 / math.sqrt(HEAD_DIM)
    qi_new = lax.broadcasted_iota(jnp.int32, (pad_rows, pad_rows), 0)
    kj_new = lax.broadcasted_iota(jnp.int32, (pad_rows, pad_rows), 1)
    row_kv = 2 * ATTN_WIDTH
    for h in range(HEADS):
        acc = m_run = l_run = None
        for g, (window, dil) in enumerate(ATTN_GROUPS):
            slope = sl_ref[g, h]
            base = (g * 3) * ATTN_WIDTH + h * HEAD_DIM
            q = qkv_ref[:, base:base + HEAD_DIM].astype(BF16)
            k_new = qkv_ref[:, base + ATTN_WIDTH:base + ATTN_WIDTH + HEAD_DIM].astype(BF16)
            v_new = qkv_ref[:, base + 2 * ATTN_WIDTH:base + 2 * ATTN_WIDTH + HEAD_DIM].astype(BF16)
            n_res = min(dil, dec_seq)
            c_ref = caches[g]
            k_c = jnp.concatenate([c_ref[:, c * row_kv + h * HEAD_DIM:c * row_kv + (h + 1) * HEAD_DIM]
                                   for c in range(n_res)], axis=0).astype(BF16)
            v_c = jnp.concatenate([c_ref[:, c * row_kv + ATTN_WIDTH + h * HEAD_DIM:
                                         c * row_kv + ATTN_WIDTH + (h + 1) * HEAD_DIM]
                                   for c in range(n_res)], axis=0).astype(BF16)
            n_keys = n_res * ATTN_CTX
            qi = lax.broadcasted_iota(jnp.int32, (pad_rows, n_keys), 0)
            kj = lax.broadcasted_iota(jnp.int32, (pad_rows, n_keys), 1)
            key_pos = (kj & (ATTN_CTX - 1)) * dil + (kj >> 7)
            dist = window + qi - key_pos
            ok_c = ((dist & (dil - 1)) == 0) & (dist <= window) & (qi < dec_seq)
            s_c = jnp.where(ok_c, _dot_nt(q, k_c) * scale - slope * dist.astype(F32), NEG_INF)
            dist_n = qi_new - kj_new
            ok_n = (dist_n >= 0) & ((dist_n & (dil - 1)) == 0) & (kj_new < dec_seq)
            s_n = jnp.where(ok_n, _dot_nt(q, k_new) * scale - slope * dist_n.astype(F32), NEG_INF)
            m = jnp.maximum(jnp.max(s_c, axis=1, keepdims=True), jnp.max(s_n, axis=1, keepdims=True))
            p_c = jnp.exp(s_c - m)
            p_n = jnp.exp(s_n - m)
            l = jnp.sum(p_c, axis=1, keepdims=True) + jnp.sum(p_n, axis=1, keepdims=True)
            a = _dot(p_c.astype(BF16), v_c) + _dot(p_n.astype(BF16), v_new)
            if g == 0:
                acc, m_run, l_run = a, m, l
            else:
                m_new = jnp.maximum(m_run, m)
                a_old = jnp.exp(m_run - m_new)
                a_cur = jnp.exp(m - m_new)
                acc = acc * a_old + a * a_cur
                l_run = l_run * a_old + l * a_cur
                m_run = m_new
        o_ref[:, h * HEAD_DIM:(h + 1) * HEAD_DIM] = (acc / l_run).astype(o_ref.dtype)


def _attn_sample(qkv_pad, caches, slopes, dec_batch, dec_seq, pad_rows):
    in_specs = [pl.BlockSpec(memory_space=pltpu.SMEM),
                pl.BlockSpec((pad_rows, qkv_pad.shape[1]), lambda n: (n, 0))]
    operands = [slopes, qkv_pad]
    for (window, dil), cache in zip(ATTN_GROUPS, caches):
        n_res = min(dil, dec_seq)
        view = cache.reshape(dec_batch, window // dil, dil * 2 * ATTN_WIDTH)
        in_specs.append(pl.BlockSpec((None, window // dil, n_res * 2 * ATTN_WIDTH), lambda n: (n, 0, 0)))
        operands.append(view)
    body = functools.partial(_attn_sample_body, dec_seq=dec_seq, pad_rows=pad_rows)
    return pl.pallas_call(
        body, grid=(dec_batch,),
        in_specs=in_specs,
        out_specs=pl.BlockSpec((pad_rows, ATTN_WIDTH), lambda n: (n, 0)),
        out_shape=jax.ShapeDtypeStruct((dec_batch * pad_rows, ATTN_WIDTH), BF16),
        compiler_params=_params("parallel"), name="attn_sample",
    )(*operands)


def _merge_body(p_ref, a_ref, wp_ref, wa_ref, gp_ref, ga_ref, o_ref):
    yp = _dot(p_ref[...], wp_ref[...])
    ya = _dot(a_ref[...], wa_ref[...])
    o_ref[...] = (jax.nn.sigmoid(gp_ref[...]) * yp + jax.nn.sigmoid(ga_ref[...]) * ya).astype(o_ref.dtype)


def _merge(pool_out, attn_out, w_pool_proj, w_attn_proj, z, tm, tn):
    rows = pool_out.shape[0]
    gp0 = GATE_OFF // tn
    ga0 = (GATE_OFF + D_MODEL) // tn
    return pl.pallas_call(
        _merge_body, grid=(rows // tm, D_MODEL // tn),
        in_specs=[pl.BlockSpec((tm, POOL_WIDTH), lambda i, j: (i, 0)),
                  pl.BlockSpec((tm, ATTN_WIDTH), lambda i, j: (i, 0)),
                  pl.BlockSpec((POOL_WIDTH, tn), lambda i, j: (0, j)),
                  pl.BlockSpec((ATTN_WIDTH, tn), lambda i, j: (0, j)),
                  pl.BlockSpec((tm, tn), lambda i, j: (i, gp0 + j)),
                  pl.BlockSpec((tm, tn), lambda i, j: (i, ga0 + j))],
        out_specs=pl.BlockSpec((tm, tn), lambda i, j: (i, j)),
        out_shape=jax.ShapeDtypeStruct((rows, D_MODEL), BF16),
        compiler_params=_params("parallel", "parallel"), name="gated_merge",
    )(pool_out, attn_out, w_pool_proj, w_attn_proj, z, z)


def _resid_body(m_ref, w_ref, x_ref, ga_ref, o_ref):
    o_ref[...] = x_ref[...] + ga_ref[...] * _dot(m_ref[...], w_ref[...])


def _resid_proj(merged, w_out, x, mod, gate_chunk, tm, tn):
    rows = merged.shape[0]
    return pl.pallas_call(
        _resid_body, grid=(rows // tm, D_MODEL // tn),
        in_specs=[pl.BlockSpec((tm, D_MODEL), lambda i, j: (i, 0)),
                  pl.BlockSpec((D_MODEL, tn), lambda i, j: (0, j)),
                  pl.BlockSpec((tm, tn), lambda i, j: (i, j)),
                  mod.spec(gate_chunk, tm, tn, ncols=2)],
        out_specs=pl.BlockSpec((tm, tn), lambda i, j: (i, j)),
        out_shape=jax.ShapeDtypeStruct((rows, D_MODEL), F32),
        compiler_params=_params("parallel", "parallel"), name="resid_proj",
    )(merged, w_out, x, mod.arr)


def _top_distinct(s, n):
    slot = lax.broadcasted_iota(jnp.int32, (n, s.shape[1]), 0)
    vals = jnp.full((n, s.shape[1]), NEG_INF, F32)
    cnts = jnp.zeros((n, s.shape[1]), F32)
    work = s
    for k in range(n):
        mx = jnp.max(work, axis=0, keepdims=True)
        eq = work == mx
        cnt = jnp.sum(jnp.where(eq, 1.0, 0.0), axis=0, keepdims=True)
        cnt = jnp.where(mx > 0.5 * NEG_INF, cnt, 0.0)
        work = jnp.where(eq, NEG_INF, work)
        vals = jnp.where(slot == k, mx, vals)
        cnts = jnp.where(slot == k, cnt, cnts)
    return vals, cnts, jnp.max(work, axis=0, keepdims=True)


def _route_body(xT_ref, wq_ref, sk_ref, thr_ref, c_ref, s1_ref, e1_ref, q_ref):
    q_ref[...] = _dot(wq_ref[...], xT_ref[...])
    k = PEER_TOPK

    def head(h, carry):
        half = []
        for p in range(2):
            row0 = pl.multiple_of((h * 2 + p) * PEER_NKEYS, PEER_NKEYS)
            q = q_ref[pl.ds(row0, PEER_NKEYS), :].astype(BF16)
            half.append(_dot(sk_ref[h * 2 + p], q))
        s0, s1 = half
        v0, n0, rest0 = _top_distinct(s0, k)
        v1, n1, rest1 = _top_distinct(s1, k)
        cand = [v0[a:a + 1, :] + v1 for a in range(k)]
        mult = [n0[a:a + 1, :] * n1 for a in range(k)]
        cand = [jnp.where(mu > 0.0, ca, NEG_INF) for ca, mu in zip(cand, mult)]
        top = v0[0:1, :] + v1[0:1, :]
        taken = jnp.zeros_like(top)
        tau = jnp.full_like(top, NEG_INF)
        below = jnp.full_like(top, NEG_INF)
        z = jnp.zeros_like(top)
        for _ in range(k + 1):
            mx = functools.reduce(jnp.maximum, [jnp.max(ca, axis=0, keepdims=True) for ca in cand])
            eqs = [ca == mx for ca in cand]
            cnt = functools.reduce(
                jnp.add, [jnp.sum(jnp.where(eq, mu, 0.0), axis=0, keepdims=True) for eq, mu in zip(eqs, mult)])
            cnt = jnp.where(mx > 0.5 * NEG_INF, cnt, 0.0)
            cand = [jnp.where(eq, NEG_INF, ca) for eq, ca in zip(eqs, cand)]
            need = taken < float(k)
            first_below = jnp.logical_and(jnp.logical_not(need), below <= 0.5 * NEG_INF)
            tau = jnp.where(need, mx, tau)
            z = z + jnp.where(need, cnt * jnp.exp(mx - top), 0.0)
            below = jnp.where(first_below, mx, below)
            taken = taken + jnp.where(need, cnt, 0.0)
        below = jnp.maximum(below, jnp.maximum(rest0 + v1[0:1, :], v0[0:1, :] + rest1))
        cut = 0.5 * (tau + below)
        thr_ref[h] = cut - s0
        c_ref[h] = jnp.exp(s0 - v0[0:1, :]) / z
        s1_ref[h] = s1
        e1_ref[h] = jnp.exp(s1 - v1[0:1, :])
        return carry

    lax.fori_loop(0, PEER_HEADS, head, 0)


def _route(xT, w_pqT, sub_keys16, tm):
    d, tokens = xT.shape
    nq = w_pqT.shape[0]
    out = jax.ShapeDtypeStruct((PEER_HEADS, PEER_NKEYS, tokens), F32)
    ospec = pl.BlockSpec((PEER_HEADS, PEER_NKEYS, tm), lambda i: (0, 0, i))
    return pl.pallas_call(
        _route_body, grid=(tokens // tm,),
        in_specs=[pl.BlockSpec((d, tm), lambda i: (0, i)),
                  pl.BlockSpec((nq, d), lambda i: (0, 0)),
                  pl.BlockSpec((2 * PEER_HEADS, PEER_NKEYS, sub_keys16.shape[2]), lambda i: (0, 0, 0))],
        out_specs=[ospec] * 4, out_shape=[out] * 4,
        scratch_shapes=[pltpu.VMEM((nq, tm), F32)],
        compiler_params=_params("parallel"), name="peer_route",
    )(xT, w_pqT, sub_keys16)


def _peer_body(xT_ref, u_ref, vT_ref, thr_ref, c_ref, s1_ref, e1_ref, o_ref):
    kb = pl.program_id(1)

    @pl.when(kb == 0)
    def _():
        o_ref[...] = jnp.zeros_like(o_ref)

    act = jax.nn.gelu(_dot(u_ref[...], xT_ref[...]))
    parts = []
    for ii in range(EXPERT_BLOCK_I):
        w = None
        for h in range(PEER_HEADS):
            sel = jnp.where(s1_ref[h] >= thr_ref[h, ii:ii + 1, :], e1_ref[h], 0.0) * c_ref[h, ii:ii + 1, :]
            w = sel if w is None else w + sel
        parts.append((act[ii * PEER_NKEYS:(ii + 1) * PEER_NKEYS, :] * w).astype(BF16))
    gated = jnp.concatenate(parts, axis=0)
    o_ref[...] += _dot(vT_ref[...], gated)


def _peer_mix(xT, peer_u, peer_vT, thr, c, s1, e1, tm):
    d, tokens = xT.shape
    eb = EXPERT_BLOCK_I * PEER_NKEYS
    n_blocks = peer_u.shape[0] // eb
    row_spec = pl.BlockSpec((PEER_HEADS, EXPERT_BLOCK_I, tm), lambda i, kb: (0, kb, i))
    col_spec = pl.BlockSpec((PEER_HEADS, PEER_NKEYS, tm), lambda i, kb: (0, 0, i))
    return pl.pallas_call(
        _peer_body, grid=(tokens // tm, n_blocks),
        in_specs=[pl.BlockSpec((d, tm), lambda i, kb: (0, i)),
                  pl.BlockSpec((eb, d), lambda i, kb: (kb, 0)),
                  pl.BlockSpec((d, eb), lambda i, kb: (0, kb)),
                  row_spec, row_spec, col_spec, col_spec],
        out_specs=pl.BlockSpec((d, tm), lambda i, kb: (0, i)),
        out_shape=jax.ShapeDtypeStruct((d, tokens), F32),
        compiler_params=_params("parallel", "arbitrary"), name="peer_mix",
    )(xT, peer_u, peer_vT, thr, c, s1, e1)


def _final_body(h_ref, pT_ref, ga_ref, g_ref, o_ref):
    h = h_ref[...] + ga_ref[...] * pT_ref[...].T
    o_ref[...] = h * lax.rsqrt(jnp.mean(h * h, axis=-1, keepdims=True) + NORM_EPS) * g_ref[...]


def _final(h, peerT, mod, gate_chunk, g_final, tm):
    rows, d = h.shape
    return pl.pallas_call(
        _final_body, grid=(rows // tm,),
        in_specs=[pl.BlockSpec((tm, d), lambda i: (i, 0)),
                  pl.BlockSpec((d, tm), lambda i: (0, i)),
                  mod.spec(gate_chunk, tm),
                  pl.BlockSpec((1, d), lambda i: (0, 0))],
        out_specs=pl.BlockSpec((tm, d), lambda i: (i, 0)),
        out_shape=jax.ShapeDtypeStruct((rows, d), F32),
        compiler_params=_params("parallel"), name="final_norm",
    )(h, peerT, mod.arr, g_final.reshape(1, d))


def _alibi_slopes():
    idx = jnp.arange(N_ATTN_GROUPS * HEADS, dtype=F32) + 1.0
    return jnp.exp2(-8.0 * idx / (N_ATTN_GROUPS * HEADS)).reshape(N_ATTN_GROUPS, HEADS)


def _peer(n2, mod, h, wts, g_final, tm_route, tm_mix, tm_final):
    xT = n2.T
    thr, c, s1, e1 = _route(xT, wts["w_pqT"], wts["sub_keys"], tm_route)
    peerT = _peer_mix(xT, wts["peer_u"], wts["peer_vT"], thr, c, s1, e1, tm_mix)
    return _final(h, peerT, mod, 5, g_final, tm_final)


def kernel(x_prompt, x_sample, state_pool, cache_kv0, cache_kv1, cache_kv2, c_prompt, c_sample, w_ada, b_ada,
           g_norm1, g_norm2, w_in, pool_w, pool_scale, w_pool_proj, w_attn_proj, w_out, w_pq, sub_keys, peer_u,
           peer_v, g_final):
    batch, seq, d = x_prompt.shape
    dec_batch, dec_seq, _ = x_sample.shape
    depth = w_ada.shape[0]
    assert depth == 1 and d == D_MODEL and seq % ATTN_TILE == 0 and dec_seq <= 8
    caches = (cache_kv0[0], cache_kv1[0], cache_kv2[0])
    for (window, _), cache in zip(ATTN_GROUPS, caches):
        assert cache.shape[1] == window

    wts = {
        "w_in": w_in[0].astype(BF16),
        "pool_w": pool_w[0].astype(BF16),
        "w_pool_proj": w_pool_proj[0].astype(BF16),
        "w_attn_proj": w_attn_proj[0].astype(BF16),
        "w_out": w_out[0].astype(BF16),
        "w_pqT": w_pq[0].T.astype(BF16),
        "sub_keys": sub_keys[0].reshape(2 * PEER_HEADS, PEER_NKEYS, -1).astype(BF16),
        "peer_u": peer_u[0].astype(BF16),
        "peer_vT": peer_v[0].T.astype(BF16),
    }
    slopes = _alibi_slopes()

    n_cond = batch + dec_batch
    cond_rows = -(-n_cond // 8) * 8
    c_all = jnp.concatenate([c_prompt, c_sample, jnp.zeros((cond_rows - n_cond, d), F32)], axis=0)
    mod_all = _adaln(c_all, w_ada[0], b_ada[0])
    mod_p = _Mod(mod_all[:batch].reshape(batch, 1, 6 * d), seq)
    s_rows = dec_batch * dec_seq
    mod_s = _Mod(jnp.repeat(mod_all[batch:n_cond], dec_seq, axis=0), None)

    xp = x_prompt.reshape(batch * seq, d)
    n1 = _norm_mod(xp, g_norm1[0], mod_p, 0, 1, 512)
    z = _matmul(n1, wts["w_in"], 1024, 1024)
    tm_pool = 256
    nt = seq // tm_pool
    pool_out = _pool(
        z, z,
        pl.BlockSpec((tm_pool, POOL_WIDTH), lambda b, i: (b * nt + i, 0)),
        pl.BlockSpec((HALO, POOL_WIDTH), lambda b, i: (jnp.maximum((b * nt + i) * (tm_pool // HALO) - 1, 0), 0)),
        wts["pool_w"], pool_scale[0], (batch, nt), tm_pool, True, 0)
    attn_out = _attn_prompt(z, slopes, batch, seq)
    merged = _merge(pool_out, attn_out, wts["w_pool_proj"], wts["w_attn_proj"], z, 512, 1024)
    hp = _resid_proj(merged, wts["w_out"], xp, mod_p, 2, 512, 1024)
    n2 = _norm_mod(hp, g_norm2[0], mod_p, 3, 4, 512)
    y_prompt = _peer(n2, mod_p, hp, wts, g_final, 256, 512, 256).reshape(batch, seq, d)

    z3 = z.reshape(batch, seq, PROJ_WIDTH)
    new_pool_p = z3[:, seq - POOL_HIST:, :POOL_WIDTH][None]
    new_kv_p = []
    for g, (window, _) in enumerate(ATTN_GROUPS):
        off = QKV_OFF + (g * 3 + 1) * ATTN_WIDTH
        keep = min(window, seq)
        new_kv_p.append(z3[:, seq - keep:, off:off + 2 * ATTN_WIDTH].reshape(batch, keep, 2, HEADS, HEAD_DIM)[None])

    xs =x_sample.reshape(s_rows, d)
    n1s = _norm_mod(xs, g_norm1[0], mod_s, 0, 1, s_rows)
    zs = _matmul(n1s, wts["w_in"], s_rows, 1024)
    zs3 = zs.reshape(dec_batch, dec_seq, PROJ_WIDTH)
    pad_rows = 16
    zs_pad = jnp.pad(zs3, ((0, 0), (0, pad_rows - dec_seq), (0, 0)))
    u_pad = zs_pad[:, :, :POOL_WIDTH].reshape(dec_batch * pad_rows, POOL_WIDTH)
    halo_s = jnp.pad(state_pool[0], ((0, 0), (HALO - POOL_HIST, 0), (0, 0))).reshape(dec_batch * HALO, POOL_WIDTH)
    pool_s = _pool(
        u_pad, halo_s,
        pl.BlockSpec((pad_rows, POOL_WIDTH), lambda b, i: (b, 0)),
        pl.BlockSpec((HALO, POOL_WIDTH), lambda b, i: (b, 0)),
        wts["pool_w"], pool_scale[0], (dec_batch, 1), pad_rows, False, PAST_LEN)
    pool_s = pool_s.reshape(dec_batch, pad_rows, POOL_WIDTH)[:, :dec_seq].reshape(s_rows, POOL_WIDTH)
    qkv_pad = zs_pad[:, :, QKV_OFF:GATE_OFF].reshape(dec_batch * pad_rows, GATE_OFF - QKV_OFF)
    attn_s = _attn_sample(qkv_pad, caches, slopes, dec_batch, dec_seq, pad_rows)
    attn_s = attn_s.reshape(dec_batch, pad_rows, ATTN_WIDTH)[:, :dec_seq].reshape(s_rows, ATTN_WIDTH)
    merged_s = _merge(pool_s, attn_s, wts["w_pool_proj"], wts["w_attn_proj"], zs, s_rows, 1024)
    hs = _resid_proj(merged_s, wts["w_out"], xs, mod_s, 2, s_rows, 1024)
    n2s = _norm_mod(hs, g_norm2[0], mod_s, 3, 4, s_rows)
    lane_rows = -(-s_rows // LANES) * LANES
    n2s_pad = jnp.pad(n2s, ((0, lane_rows - s_rows), (0, 0)))
    hs_pad = jnp.pad(hs, ((0, lane_rows - s_rows), (0, 0)))
    mod_s_pad = _Mod(jnp.pad(mod_s.arr, ((0, lane_rows - s_rows), (0, 0))), None)
    y_sample = _peer(n2s_pad, mod_s_pad, hs_pad, wts, g_final, lane_rows, lane_rows, lane_rows)
    y_sample = y_sample[:s_rows].reshape(dec_batch, dec_seq, d)

    new_pool_s = jnp.concatenate([state_pool[0], zs3[:, :, :POOL_WIDTH]], axis=1)[:, -POOL_HIST:][None]
    new_kv_s = []
    for g, ((window, _), cache) in enumerate(zip(ATTN_GROUPS, caches)):
        off = QKV_OFF + (g * 3 + 1) * ATTN_WIDTH
        new_rows = zs3[:, :, off:off + 2 * ATTN_WIDTH].reshape(dec_batch, dec_seq, 2, HEADS, HEAD_DIM)
        new_kv_s.append(jnp.concatenate([cache, new_rows], axis=1)[:, -window:][None])

    return (y_prompt, y_sample, new_pool_p, new_kv_p[0], new_kv_p[1], new_kv_p[2],
            new_pool_s, new_kv_s[0], new_kv_s[1], new_kv_s[2])
```

```python
import functools
import math

import jax
import jax.numpy as jnp
from jax import lax
from jax.experimental import pallas as pl
from jax.experimental.pallas import tpu as pltpu

F32 = jnp.float32
BF16 = jnp.bfloat16

D_MODEL = 2048
POOL_WINDOWS = (2, 4, 8, 16)
POOL_GROUPS = len(POOL_WINDOWS)
POOL_WIDTH = D_MODEL // 2
POOL_GROUP_WIDTH = POOL_WIDTH // POOL_GROUPS
POOL_HIST = max(POOL_WINDOWS) - 1
ATTN_GROUPS = ((128, 1), (512, 4), (2048, 16))
N_ATTN_GROUPS = len(ATTN_GROUPS)
HEADS = 8
HEAD_DIM = 128
ATTN_WIDTH = HEADS * HEAD_DIM
ATTN_CTX = 128
QKV_OFF = POOL_WIDTH
GATE_OFF = QKV_OFF + N_ATTN_GROUPS * 3 * ATTN_WIDTH
PROJ_WIDTH = GATE_OFF + 2 * D_MODEL
PEER_HEADS = 8
PEER_NKEYS = 128
PEER_TOPK = 16
NORM_EPS = 1e-6
NEG_INF = -1e30
PAST_LEN = 16384

LANES = 128
SUBLANES = 8
MXU_WIDTH = 256
HALO = 16
VMEM_LIMIT = 56 * 1024 * 1024
ATTN_TILE = 2048
EXPERT_BLOCK_I = 4


def _params(*sem):
    return pltpu.CompilerParams(dimension_semantics=sem, vmem_limit_bytes=VMEM_LIMIT)


def _dot(a, b):
    return jnp.dot(a, b, preferred_element_type=F32)


def _dot_nt(a, b):
    return lax.dot_general(a, b, (((1,), (1,)), ((), ())), preferred_element_type=F32)


def _ada_body(c_ref, w_ref, b_ref, o_ref):
    c = c_ref[...]
    s = c * jax.nn.sigmoid(c)
    o_ref[...] = _dot(s.astype(BF16), w_ref[...].astype(BF16)) + b_ref[...]


def _adaln(c_all, w_ada, b_ada):
    rows, d = c_all.shape
    n = w_ada.shape[1]
    tn = 1024
    return pl.pallas_call(
        _ada_body, grid=(n // tn,),
        in_specs=[pl.BlockSpec((rows, d), lambda j: (0, 0)),
                  pl.BlockSpec((d, tn), lambda j: (0, j)),
                  pl.BlockSpec((1, tn), lambda j: (0, j))],
        out_specs=pl.BlockSpec((rows, tn), lambda j: (0, j)),
        out_shape=jax.ShapeDtypeStruct((rows, n), F32),
        compiler_params=_params("parallel"), name="adaln",
    )(c_all, w_ada, b_ada.reshape(1, n))


class _Mod:
    def __init__(self, arr, rows_per_group):
        self.arr = arr
        self.rows_per_group = rows_per_group

    def spec(self, chunk, tm, tn=D_MODEL, ncols=1):
        per = D_MODEL // tn
        if self.rows_per_group is None:
            if ncols == 1:
                return pl.BlockSpec((tm, tn), lambda i: (i, chunk * per))
            return pl.BlockSpec((tm, tn), lambda i, j: (i, chunk * per + j))
        tpg = self.rows_per_group // tm
        if ncols == 1:
            return pl.BlockSpec((None, 1, tn), lambda i: (i // tpg, 0, chunk * per))
        return pl.BlockSpec((None, 1, tn), lambda i, j: (i // tpg, 0, chunk * per + j))


def _norm_mod_body(x_ref, g_ref, sc_ref, sh_ref, o_ref):
    x = x_ref[...]
    y = x * lax.rsqrt(jnp.mean(x * x, axis=-1, keepdims=True) + NORM_EPS) * g_ref[...]
    o_ref[...] = (y * (1.0 + sc_ref[...]) + sh_ref[...]).astype(o_ref.dtype)


def _norm_mod(x, gain, mod, shift_chunk, scale_chunk, tm):
    rows, d = x.shape
    return pl.pallas_call(
        _norm_mod_body, grid=(rows // tm,),
        in_specs=[pl.BlockSpec((tm, d), lambda i: (i, 0)),
                  pl.BlockSpec((1, d), lambda i: (0, 0)),
                  mod.spec(scale_chunk, tm), mod.spec(shift_chunk, tm)],
        out_specs=pl.BlockSpec((tm, d), lambda i: (i, 0)),
        out_shape=jax.ShapeDtypeStruct((rows, d), BF16),
        compiler_params=_params("parallel"), name="norm_mod",
    )(x, gain.reshape(1, d), mod.arr, mod.arr)


def _mm_body(x_ref, w_ref, o_ref):
    o_ref[...] = _dot(x_ref[...], w_ref[...]).astype(o_ref.dtype)


def _matmul(x, w, tm, tn, out_dtype=F32):
    rows, k = x.shape
    n = w.shape[1]
    return pl.pallas_call(
        _mm_body, grid=(rows // tm, n // tn),
        in_specs=[pl.BlockSpec((tm, k), lambda i, j: (i, 0)),
                  pl.BlockSpec((k, tn), lambda i, j: (0, j))],
        out_specs=pl.BlockSpec((tm, tn), lambda i, j: (i, j)),
        out_shape=jax.ShapeDtypeStruct((rows, n), out_dtype),
        compiler_params=_params("parallel", "parallel"), name="in_proj",
    )(x, w)


def _pool_body(u_ref, h_ref, pw_ref, ps_ref, o_ref, ext_ref, *, tm, zero_first, pos0):
    i = pl.program_id(1)
    halo = h_ref[...]
    if zero_first:
        halo = jnp.where(i == 0, 0.0, halo)
    ext_ref[0:HALO, :] = halo
    ext_ref[HALO:HALO + tm, :] = u_ref[...]
    pos = pos0 + i * tm + lax.broadcasted_iota(jnp.int32, (tm, 1), 0)
    for g, w in enumerate(POOL_WINDOWS):
        cols = slice(g * POOL_GROUP_WIDTH, (g + 1) * POOL_GROUP_WIDTH)
        tok = ext_ref[HALO:HALO + tm, cols]
        win = tok
        for back in range(1, w):
            win = win + ext_ref[HALO - back:HALO - back + tm, cols]
        count = jnp.minimum(w, pos + 1).astype(F32)
        pooled = win / count - tok
        mixed = _dot(pooled.astype(BF16), pw_ref[g])
        o_ref[:, cols] = (mixed * ps_ref[:, cols]).astype(o_ref.dtype)


def _pool(u_arr, halo_arr, u_spec, halo_spec, pool_w, pool_scale, grid, tm, zero_first, pos0):
    rows = grid[0] * grid[1] * tm
    body = functools.partial(_pool_body, tm=tm, zero_first=zero_first, pos0=pos0)
    return pl.pallas_call(
        body, grid=grid,
        in_specs=[u_spec, halo_spec,
                  pl.BlockSpec((POOL_GROUPS, POOL_GROUP_WIDTH, POOL_GROUP_WIDTH), lambda b, i: (0, 0, 0)),
                  pl.BlockSpec((1, POOL_WIDTH), lambda b, i: (0, 0))],
        out_specs=pl.BlockSpec((tm, POOL_WIDTH), lambda b, i: (b * grid[1] + i, 0)),
        out_shape=jax.ShapeDtypeStruct((rows, POOL_WIDTH), BF16),
        scratch_shapes=[pltpu.VMEM((HALO + tm, POOL_WIDTH), F32)],
        compiler_params=_params("parallel", "parallel"), name="pool_mix",
    )(u_arr, halo_arr, pool_w, pool_scale.reshape(1, POOL_WIDTH))


def _rows(start, dil):
    return pl.ds(start, ATTN_CTX, stride=dil) if dil > 1 else pl.ds(start, ATTN_CTX)


def _attn_prompt_body(sl_ref, *refs):
    ins, o_ref, scr = refs[:15], refs[15], refs[16:]
    acc_ref, m_ref, l_ref = scr[6:]
    t = pl.program_id(1)
    head = pl.program_id(2)
    steps_cur =(lax.broadcasted_iota(jnp.int32, (ATTN_CTX, ATTN_CTX), 0)
                 - lax.broadcasted_iota(jnp.int32, (ATTN_CTX, ATTN_CTX), 1)).astype(F32)
    steps_prev = steps_cur + float(ATTN_CTX)
    scale = 1.0 / math.sqrt(HEAD_DIM)
    for g, (window, dil) in enumerate(ATTN_GROUPS):
        q_ref, k_ref, v_ref, kp_ref, vp_ref = ins[5 * g:5 * g + 5]
        kx_ref, vx_ref = scr[2 * g], scr[2 * g + 1]
        kx_ref[0:window, :] = kp_ref[...]
        vx_ref[0:window, :] = vp_ref[...]
        kx_ref[window:window + ATTN_TILE, :] = k_ref[...]
        vx_ref[window:window + ATTN_TILE, :] = v_ref[...]
        slope = sl_ref[g, head] * float(dil)
        bias_cur = slope * steps_cur
        bias_prev = slope * steps_prev

        def block(idx, carry, g=g, window=window, dil=dil, q_ref=q_ref, kx_ref=kx_ref, vx_ref=vx_ref,
                  bias_cur=bias_cur, bias_prev=bias_prev):
            n = idx // dil
            r = idx % dil
            start = n * window + r
            q = q_ref[_rows(start, dil), :].astype(BF16)
            kc = kx_ref[_rows(window + start, dil), :].astype(BF16)
            vc = vx_ref[_rows(window + start, dil), :].astype(BF16)
            kp = kx_ref[_rows(start, dil), :].astype(BF16)
            vp = vx_ref[_rows(start, dil), :].astype(BF16)
            prev_limit = jnp.where(jnp.logical_or(t > 0, n > 0), float(ATTN_CTX), -1.0)
            s_cur = jnp.where(steps_cur >= 0.0, _dot_nt(q, kc) * scale - bias_cur, NEG_INF)
            s_prev = jnp.where(steps_prev <= prev_limit, _dot_nt(q, kp) * scale - bias_prev, NEG_INF)
            m = jnp.maximum(jnp.max(s_cur, axis=1, keepdims=True), jnp.max(s_prev, axis=1, keepdims=True))
            p_cur = jnp.exp(s_cur - m)
            p_prev = jnp.exp(s_prev - m)
            l = jnp.sum(p_cur, axis=1, keepdims=True) + jnp.sum(p_prev, axis=1, keepdims=True)
            acc = _dot(p_cur.astype(BF16), vc) + _dot(p_prev.astype(BF16), vp)
            m = jnp.broadcast_to(m, (ATTN_CTX, LANES))
            l = jnp.broadcast_to(l, (ATTN_CTX, LANES))
            rows = _rows(start, dil)
            if g > 0:
                m_old = m_ref[rows, :]
                m_new = jnp.maximum(m_old, m)
                a_old = jnp.exp(m_old - m_new)
                a_new = jnp.exp(m - m_new)
                acc = acc_ref[rows, :] * a_old + acc * a_new
                l = l_ref[rows, :] * a_old + l * a_new
                m = m_new
            acc_ref[rows, :] = acc
            m_ref[rows, :] = m
            l_ref[rows, :] = l
            return carry

        lax.fori_loop(0, ATTN_TILE // ATTN_CTX, block, 0)
    o_ref[...] = (acc_ref[...] / l_ref[...]).astype(o_ref.dtype)


def _attn_prompt(z, slopes, batch, seq):
    tiles = seq // ATTN_TILE
    col0 = QKV_OFF // HEAD_DIM
    in_specs = [pl.BlockSpec(memory_space=pltpu.SMEM)]
    operands = [slopes]
    scratch = []
    for g, (window, _) in enumerate(ATTN_GROUPS):
        per_tile = ATTN_TILE // window
        for part in range(3):
            col = col0 + (g * 3 + part) * HEADS
            in_specs.append(pl.BlockSpec((ATTN_TILE, HEAD_DIM),
                                         lambda b, t, h, col=col: (b * tiles + t, col + h)))
            operands.append(z)
        for part in (1, 2):
            col = col0 + (g * 3 + part) * HEADS
            in_specs.append(pl.BlockSpec(
                (window, HEAD_DIM),
                lambda b, t, h, col=col, per_tile=per_tile:
                (jnp.maximum((b * tiles + t) * per_tile - 1, 0), col + h)))
            operands.append(z)
        scratch += [pltpu.VMEM((window + ATTN_TILE, HEAD_DIM), F32)] * 2
    scratch += [pltpu.VMEM((ATTN_TILE, LANES), F32)] * 3
    return pl.pallas_call(
        _attn_prompt_body, grid=(batch, tiles, HEADS),
        in_specs=in_specs,
        out_specs=pl.BlockSpec((ATTN_TILE, HEAD_DIM), lambda b, t, h: (b * tiles + t, h)),
        out_shape=jax.ShapeDtypeStruct((batch * seq, ATTN_WIDTH), BF16),
        scratch_shapes=scratch,
        compiler_params=_params("parallel", "parallel", "parallel"), name="attn_prompt",
    )(*operands)


def _attn_sample_body(sl_ref, qkv_ref, c0_ref, c1_ref, c2_ref, o_ref, *, dec_seq, pad_rows):
    caches = (c0_ref, c1_ref, c2_ref)
    scale = 1.0 / math.sqrt(HEAD_DIM)
    qi_new = lax.broadcasted_iota(jnp.int32, (pad_rows, pad_rows), 0)
    kj_new = lax.broadcasted_iota(jnp.int32, (pad_rows, pad_rows), 1)
    row_kv = 2 * ATTN_WIDTH
    for h in range(HEADS):
        acc = m_run = l_run = None
        for g, (window, dil) in enumerate(ATTN_GROUPS):
            slope = sl_ref[g, h]
            base = (g * 3) * ATTN_WIDTH + h * HEAD_DIM
            q = qkv_ref[:, base:base + HEAD_DIM].astype(BF16)
            k_new = qkv_ref[:, base + ATTN_WIDTH:base + ATTN_WIDTH + HEAD_DIM].astype(BF16)
            v_new = qkv_ref[:, base + 2 * ATTN_WIDTH:base + 2 * ATTN_WIDTH + HEAD_DIM].astype(BF16)
            n_res = min(dil, dec_seq)
            c_ref = caches[g]
            k_c = jnp.concatenate([c_ref[:, c * row_kv + h * HEAD_DIM:c * row_kv + (h + 1) * HEAD_DIM]
                                   for c in range(n_res)], axis=0).astype(BF16)
            v_c = jnp.concatenate([c_ref[:, c * row_kv + ATTN_WIDTH + h * HEAD_DIM:
                                         c * row_kv + ATTN_WIDTH + (h + 1) * HEAD_DIM]
                                   for c in range(n_res)], axis=0).astype(BF16)
            n_keys = n_res * ATTN_CTX
            qi = lax.broadcasted_iota(jnp.int32, (pad_rows, n_keys), 0)
            kj = lax.broadcasted_iota(jnp.int32, (pad_rows, n_keys), 1)
            key_pos = (kj & (ATTN_CTX - 1)) * dil + (kj >> 7)
            dist = window + qi - key_pos
            ok_c = ((dist & (dil - 1)) == 0) & (dist <= window) & (qi < dec_seq)
            s_c = jnp.where(ok_c, _dot_nt(q, k_c) * scale - slope * dist.astype(F32), NEG_INF)
            dist_n = qi_new - kj_new
            ok_n = (dist_n >= 0) & ((dist_n & (dil - 1)) == 0) & (kj_new < dec_seq)
            s_n = jnp.where(ok_n, _dot_nt(q, k_new) * scale - slope * dist_n.astype(F32), NEG_INF)
            m = jnp.maximum(jnp.max(s_c, axis=1, keepdims=True), jnp.max(s_n, axis=1, keepdims=True))
            p_c = jnp.exp(s_c - m)
            p_n = jnp.exp(s_n - m)
            l = jnp.sum(p_c, axis=1, keepdims=True) + jnp.sum(p_n, axis=1, keepdims=True)
            a = _dot(p_c.astype(BF16), v_c) + _dot(p_n.astype(BF16), v_new)
            if g == 0:
                acc, m_run, l_run = a, m, l
            else:
                m_new = jnp.maximum(m_run, m)
                a_old = jnp.exp(m_run - m_new)
                a_cur = jnp.exp(m - m_new)
                acc = acc * a_old + a * a_cur
                l_run = l_run * a_old + l * a_cur
                m_run = m_new
        o_ref[:, h * HEAD_DIM:(h + 1) * HEAD_DIM] = (acc / l_run).astype(o_ref.dtype)


def _attn_sample(qkv_pad, caches, slopes, dec_batch, dec_seq, pad_rows):
    in_specs = [pl.BlockSpec(memory_space=pltpu.SMEM),
                pl.BlockSpec((pad_rows, qkv_pad.shape[1]), lambda n: (n, 0))]
    operands = [slopes, qkv_pad]
    for (window, dil), cache in zip(ATTN_GROUPS, caches):
        n_res = min(dil, dec_seq)
        view = cache.reshape(dec_batch, window // dil, dil * 2 * ATTN_WIDTH)
        in_specs.append(pl.BlockSpec((None, window // dil, n_res * 2 * ATTN_WIDTH), lambda n: (n, 0, 0)))
        operands.append(view)
    body = functools.partial(_attn_sample_body, dec_seq=dec_seq, pad_rows=pad_rows)
    return pl.pallas_call(
        body, grid=(dec_batch,),
        in_specs=in_specs,
        out_specs=pl.BlockSpec((pad_rows, ATTN_WIDTH), lambda n: (n, 0)),
        out_shape=jax.ShapeDtypeStruct((dec_batch * pad_rows, ATTN_WIDTH), BF16),
        compiler_params=_params("parallel"), name="attn_sample",
    )(*operands)


def _merge_body(p_ref, a_ref, wp_ref, wa_ref, gp_ref, ga_ref, o_ref):
    yp = _dot(p_ref[...], wp_ref[...])
    ya = _dot(a_ref[...], wa_ref[...])
    o_ref[...] = (jax.nn.sigmoid(gp_ref[...]) * yp + jax.nn.sigmoid(ga_ref[...]) * ya).astype(o_ref.dtype)


def _merge(pool_out, attn_out, w_pool_proj, w_attn_proj, z, tm, tn):
    rows = pool_out.shape[0]
    gp0 = GATE_OFF // tn
    ga0 = (GATE_OFF + D_MODEL) // tn
    return pl.pallas_call(
        _merge_body, grid=(rows // tm, D_MODEL // tn),
        in_specs=[pl.BlockSpec((tm, POOL_WIDTH), lambda i, j: (i, 0)),
                  pl.BlockSpec((tm, ATTN_WIDTH), lambda i, j: (i, 0)),
                  pl.BlockSpec((POOL_WIDTH, tn), lambda i, j: (0, j)),
                  pl.BlockSpec((ATTN_WIDTH, tn), lambda i, j: (0, j)),
                  pl.BlockSpec((tm, tn), lambda i, j: (i, gp0 + j)),
                  pl.BlockSpec((tm, tn), lambda i, j: (i, ga0 + j))],
        out_specs=pl.BlockSpec((tm, tn), lambda i, j: (i, j)),
        out_shape=jax.ShapeDtypeStruct((rows, D_MODEL), BF16),
        compiler_params=_params("parallel", "parallel"), name="gated_merge",
    )(pool_out, attn_out, w_pool_proj, w_attn_proj, z, z)


def _resid_body(m_ref, w_ref, x_ref, ga_ref, o_ref):
    o_ref[...] = x_ref[...] + ga_ref[...] * _dot(m_ref[...], w_ref[...])


def _resid_proj(merged, w_out, x, mod, gate_chunk, tm, tn):
    rows = merged.shape[0]
    return pl.pallas_call(
        _resid_body, grid=(rows // tm, D_MODEL // tn),
        in_specs=[pl.BlockSpec((tm, D_MODEL), lambda i, j: (i, 0)),
                  pl.BlockSpec((D_MODEL, tn), lambda i, j: (0, j)),
                  pl.BlockSpec((tm, tn), lambda i, j: (i, j)),
                  mod.spec(gate_chunk, tm, tn, ncols=2)],
        out_specs=pl.BlockSpec((tm, tn), lambda i, j: (i, j)),
        out_shape=jax.ShapeDtypeStruct((rows, D_MODEL), F32),
        compiler_params=_params("parallel", "parallel"), name="resid_proj",
    )(merged, w_out, x, mod.arr)


def _odd_even_merge_sort(n):
    pairs = []
    p = 1
    while p < n:
        k = p
        while k >= 1:
            for j in range(k % p, n - k, 2 * k):
                for i in range(min(k, n - j - k)):
                    if (i + j) // (2 * p) == (i + j + k) // (2 * p):
                        pairs.append((i + j, i + j + k))
            k //= 2
        p *= 2
    return pairs


def _top16(s):
    k = PEER_TOPK
    x = [s[SUBLANES * i:SUBLANES * (i + 1), :] for i in range(PEER_NKEYS // SUBLANES)]
    for i, j in _odd_even_merge_sort(k):
        x[i], x[j] = jnp.maximum(x[i], x[j]), jnp.minimum(x[i], x[j])
    dropped = jnp.full_like(x[0], NEG_INF)
    shift = SUBLANES // 2
    while shift >= 1:
        y = [pltpu.roll(v, shift, axis=0) for v in x]
        dropped = jnp.maximum(dropped, pltpu.roll(dropped, shift, axis=0))
        lo = [jnp.minimum(x[i], y[k - 1 - i]) for i in range(k)]
        dropped = functools.reduce(jnp.maximum, lo, dropped)
        x = [jnp.maximum(x[i], y[k - 1 - i]) for i in range(k)]
        stride = k // 2
        while stride >= 1:
            for i in range(k):
                if (i // stride) % 2 == 0:
                    j = i + stride
                    x[i], x[j] = jnp.maximum(x[i], x[j]), jnp.minimum(x[i], x[j])
            stride //= 2
        shift //= 2
    return x, dropped


def _gate_rows(ref, h, row):
    r = ref[h, row:row + 1, :]
    return jnp.broadcast_to(r, (2 * SUBLANES, r.shape[1])).astype(BF16)


def _route_body(xT_ref, wq_ref, sk_ref, n_ref, c_ref, rank_ref, e1_ref, q_ref):
    q_ref[...] = _dot(wq_ref[...], xT_ref[...])
    k = PEER_TOPK

    def head(h, carry):
        half = []
        for p in range(2):
            row0 = pl.multiple_of((h * 2 + p) * PEER_NKEYS, PEER_NKEYS)
            q = q_ref[pl.ds(row0, PEER_NKEYS), :].astype(BF16)
            half.append(_dot(sk_ref[h * 2 + p], q))
        s0, s1 = half
        v0, rest0 = _top16(s0)
        v1, rest1 = _top16(s1)
        row = lax.broadcasted_iota(jnp.int32, v0[0].shape, 0)

        def along_sublanes(vals):
            return functools.reduce(lambda acc, b: jnp.where(row == b, vals[b], acc), range(1, SUBLANES), vals[0])

        v1_lo, v1_hi, v0_hi = along_sublanes(v1[:SUBLANES]), along_sublanes(v1[SUBLANES:]), along_sublanes(v0[SUBLANES:])
        cand = [v0[0] + v1_lo, v0[0] + v1_hi, v0_hi + v1[0]]
        for a in range(1, SUBLANES):
            cand.append(jnp.where(row <= (k + 1) // (a + 1) - 1, v0[a] + v1_lo, NEG_INF))
        top = (v0[0] + v1[0])[0:1, :]
        taken = jnp.zeros_like(top)
        tau = jnp.full_like(top, NEG_INF)
        below = jnp.full_like(top, NEG_INF)
        z = jnp.zeros_like(top)
        for _ in range(k + 1):
            mx = jnp.max(functools.reduce(jnp.maximum, cand), axis=0, keepdims=True)
            eqs = [ca == mx for ca in cand]
            cnt = jnp.sum(functools.reduce(jnp.add, [jnp.where(eq, 1.0, 0.0) for eq in eqs]), axis=0, keepdims=True)
            cand = [jnp.where(eq, NEG_INF, ca) for eq, ca in zip(eqs, cand)]
            need = taken < float(k)
            first_below = jnp.logical_and(jnp.logical_not(need), below <= 0.5 * NEG_INF)
            tau = jnp.where(need, mx, tau)
            z = z + jnp.where(need, cnt * jnp.exp(mx - top), 0.0)
            below = jnp.where(first_below, mx, below)
            taken = taken + jnp.where(need, cnt, 0.0)
        below = jnp.maximum(below, jnp.maximum(rest0 + v1[0], v0[0] + rest1)[0:1, :])
        cut = 0.5 * (tau + below)
        rank1 = jnp.zeros_like(s1)
        n_pass = jnp.zeros_like(s0)
        for b in range(k):
            v1b = v1[b][0:1, :]
            rank1 = rank1 + jnp.where(s1 < v1b, 1.0, 0.0)
            n_pass = n_pass + jnp.where(s0 + v1b >= cut, 1.0, 0.0)
        n_ref[h] = n_pass
        c_ref[h] = jnp.exp(s0 - v0[0][0:1, :]) / z
        rank_ref[h] = rank1.astype(rank_ref.dtype)
        e1_ref[h] = jnp.exp(s1 - v1[0][0:1, :]).astype(e1_ref.dtype)
        return carry

    lax.fori_loop(0, PEER_HEADS, head, 0)


def _route(xT, w_pqT, sub_keys16, tm):
    d, tokens = xT.shape
    nq = w_pqT.shape[0]
    shape = (PEER_HEADS, PEER_NKEYS, tokens)
    ospec = pl.BlockSpec((PEER_HEADS, PEER_NKEYS, tm), lambda i: (0, 0, i))
    return pl.pallas_call(
        _route_body, grid=(tokens // tm,),
        in_specs=[pl.BlockSpec((d, tm), lambda i: (0, i)),
                  pl.BlockSpec((nq, d), lambda i: (0, 0)),
                  pl.BlockSpec((2 * PEER_HEADS, PEER_NKEYS, sub_keys16.shape[2]), lambda i: (0, 0, 0))],
        out_specs=[ospec] * 4,
        out_shape=[jax.ShapeDtypeStruct(shape, F32), jax.ShapeDtypeStruct(shape, F32),
                   jax.ShapeDtypeStruct(shape, BF16), jax.ShapeDtypeStruct(shape, BF16)],
        scratch_shapes=[pltpu.VMEM((nq, tm), F32)],
        compiler_params=_params("parallel"), name="peer_route",
    )(xT, w_pqT, sub_keys16)


def _peer_body(xT_ref, u_ref, vT_ref, na_ref, ca_ref, nb_ref, cb_ref, rank_ref, e1_ref, o_ref, act_ref, gated_ref):
    s = pl.program_id(1)
    eb = EXPERT_BLOCK_I * PEER_NKEYS

    @pl.when(s == 0)
    def _():
        o_ref[...] = jnp.zeros_like(o_ref)
        act_ref[...] = jnp.zeros_like(act_ref)
        gated_ref[...] = jnp.zeros_like(gated_ref)

    def stage_a(rows):
        return jax.nn.gelu(_dot(u_ref[rows, :], xT_ref[...])).astype(BF16)

    chunk = 16
    n_chunks = PEER_NKEYS // chunk

    def stage_b(act, n_ref, c_ref, row0):
        parts = []
        for ii in range(EXPERT_BLOCK_I):
            w = [None] * n_chunks
            for h in range(PEER_HEADS):
                n_rows = _gate_rows(n_ref, h, row0 + ii)
                c_rows = _gate_rows(c_ref, h, row0 + ii)
                for jc in range(n_chunks):
                    keys = slice(jc * chunk, (jc + 1) * chunk)
                    sel = jnp.where(rank_ref[h, keys, :] < n_rows, e1_ref[h, keys, :], 0.0) * c_rows
                    w[jc] = sel if w[jc] is None else w[jc] + sel
            base = ii * PEER_NKEYS
            parts += [act[base + jc * chunk:base + (jc + 1) * chunk, :] * w[jc] for jc in range(n_chunks)]
        return jnp.concatenate(parts, axis=0)

    read, write = (s + 1) % 2, s % 2
    act0 = stage_a(slice(0, eb))
    gated1 = stage_b(act_ref[read], na_ref, ca_ref, EXPERT_BLOCK_I)
    act1 = stage_a(slice(eb, 2 * eb))
    gated0 = stage_b(act0, nb_ref, cb_ref, 0)
    o_ref[...] += _dot(vT_ref[...], jnp.concatenate([gated_ref[read], gated1], axis=0))
    act_ref[write] = act1
    gated_ref[write] = gated0


def _peer_mix(xT, peer_u, peer_vT, n_pass, c, rank1, e1, tm):
    d, tokens = xT.shape
    eb = EXPERT_BLOCK_I * PEER_NKEYS
    steps = peer_u.shape[0] // (2 * eb)
    prev_rows = pl.BlockSpec((PEER_HEADS, 2 * EXPERT_BLOCK_I, tm), lambda i, s: (0, jnp.maximum(s - 1, 0), i))
    cur_rows = pl.BlockSpec((PEER_HEADS, 2 * EXPERT_BLOCK_I, tm), lambda i, s: (0, jnp.minimum(s, steps - 1), i))
    col_spec = pl.BlockSpec((PEER_HEADS, PEER_NKEYS, tm), lambda i, s: (0, 0, i))
    return pl.pallas_call(
        _peer_body, grid=(tokens // tm, steps + 1),
        in_specs=[pl.BlockSpec((d, tm), lambda i, s: (0, i)),
                  pl.BlockSpec((2 * eb, d), lambda i, s: (jnp.minimum(s, steps - 1), 0)),
                  pl.BlockSpec((d, 2 * eb), lambda i, s: (0, jnp.maximum(s - 1, 0))),
                  prev_rows, prev_rows, cur_rows, cur_rows, col_spec, col_spec],
        out_specs=pl.BlockSpec((d, tm), lambda i, s: (0, i)),
        out_shape=jax.ShapeDtypeStruct((d, tokens), F32),
        scratch_shapes=[pltpu.VMEM((2, eb, tm), BF16), pltpu.VMEM((2, eb, tm), BF16)],
        compiler_params=_params("parallel", "arbitrary"), name="peer_mix",
    )(xT, peer_u, peer_vT, n_pass, c, n_pass, c, rank1, e1)


def _final_body(h_ref, pT_ref, ga_ref, g_ref, o_ref):
    h = h_ref[...] + ga_ref[...] * pT_ref[...].T
    o_ref[...] = h * lax.rsqrt(jnp.mean(h * h, axis=-1, keepdims=True) + NORM_EPS) * g_ref[...]


def _final(h, peerT, mod, gate_chunk, g_final, tm):
    rows, d = h.shape
    return pl.pallas_call(
        _final_body, grid=(rows // tm,),
        in_specs=[pl.BlockSpec((tm, d), lambda i: (i, 0)),
                  pl.BlockSpec((d, tm), lambda i: (0, i)),
                  mod.spec(gate_chunk, tm),
                  pl.BlockSpec((1, d), lambda i: (0, 0))],
        out_specs=pl.BlockSpec((tm, d), lambda i: (i, 0)),
        out_shape=jax.ShapeDtypeStruct((rows, d), F32),
        compiler_params=_params("parallel"), name="final_norm",
    )(h, peerT, mod.arr, g_final.reshape(1, d))


def _alibi_slopes():
    idx = jnp.arange(N_ATTN_GROUPS * HEADS, dtype=F32) + 1.0
    return jnp.exp2(-8.0 * idx / (N_ATTN_GROUPS * HEADS)).reshape(N_ATTN_GROUPS, HEADS)


def _peer(n2, mod, h, wts, g_final, tm_route, tm_mix, tm_final):
    xT = n2.T
    n_pass, c, rank1, e1 = _route(xT, wts["w_pqT"], wts["sub_keys"], tm_route)
    peerT = _peer_mix(xT, wts["peer_u"], wts["peer_vT"], n_pass, c, rank1, e1, tm_mix)
    return _final(h, peerT, mod, 5, g_final, tm_final)


def kernel(x_prompt, x_sample, state_pool, cache_kv0, cache_kv1, cache_kv2, c_prompt, c_sample, w_ada, b_ada,
           g_norm1, g_norm2, w_in, pool_w, pool_scale, w_pool_proj, w_attn_proj, w_out, w_pq, sub_keys, peer_u,
           peer_v, g_final):
    batch, seq, d = x_prompt.shape
    dec_batch, dec_seq, _ = x_sample.shape
    depth = w_ada.shape[0]
    assert depth == 1 and d == D_MODEL and seq % ATTN_TILE == 0 and dec_seq <= 8
    caches = (cache_kv0[0], cache_kv1[0], cache_kv2[0])
    for (window, _), cache in zip(ATTN_GROUPS, caches):
        assert cache.shape[1] == window

    wts = {
        "w_in": w_in[0].astype(BF16),
        "pool_w": pool_w[0].astype(BF16),
        "w_pool_proj": w_pool_proj[0].astype(BF16),
        "w_attn_proj": w_attn_proj[0].astype(BF16),
        "w_out": w_out[0].astype(BF16),
        "w_pqT": w_pq[0].T.astype(BF16),
        "sub_keys": sub_keys[0].reshape(2 * PEER_HEADS, PEER_NKEYS, -1).astype(BF16),
        "peer_u": peer_u[0].astype(BF16),
        "peer_vT": peer_v[0].T.astype(BF16),
    }
    slopes = _alibi_slopes()

    n_cond = batch + dec_batch
    cond_rows = -(-n_cond // 8) * 8
    c_all = jnp.concatenate([c_prompt, c_sample, jnp.zeros((cond_rows - n_cond, d), F32)], axis=0)
    mod_all = _adaln(c_all, w_ada[0], b_ada[0])
    mod_p = _Mod(mod_all[:batch].reshape(batch, 1, 6 * d), seq)
    s_rows = dec_batch * dec_seq
    mod_s = _Mod(jnp.repeat(mod_all[batch:n_cond], dec_seq, axis=0), None)

    xp = x_prompt.reshape(batch * seq, d)
    n1 = _norm_mod(xp, g_norm1[0], mod_p, 0, 1, 512)
    z = _matmul(n1, wts["w_in"], 1024, 1024)
    tm_pool = 256
    nt = seq // tm_pool
    pool_out = _pool(
        z, z,
        pl.BlockSpec((tm_pool, POOL_WIDTH), lambda b, i: (b * nt + i, 0)),
        pl.BlockSpec((HALO, POOL_WIDTH), lambda b, i: (jnp.maximum((b * nt + i) * (tm_pool // HALO) - 1, 0), 0)),
        wts["pool_w"], pool_scale[0], (batch, nt), tm_pool, True, 0)
    attn_out = _attn_prompt(z, slopes, batch, seq)
    merged = _merge(pool_out, attn_out, wts["w_pool_proj"], wts["w_attn_proj"], z, 512, 1024)
    hp = _resid_proj(merged, wts["w_out"], xp, mod_p, 2, 512, 1024)
    n2 = _norm_mod(hp, g_norm2[0], mod_p, 3, 4, 512)
    y_prompt = _peer(n2, mod_p, hp, wts, g_final, 256, 512, 256).reshape(batch, seq, d)

    z3 = z.reshape(batch, seq, PROJ_WIDTH)
    new_pool_p = z3[:, seq - POOL_HIST:, :POOL_WIDTH][None]
    new_kv_p = []
    for g, (window, _) in enumerate(ATTN_GROUPS):
        off = QKV_OFF + (g * 3 + 1) * ATTN_WIDTH
        keep = min(window, seq)
        new_kv_p.append(z3[:, seq - keep:, off:off + 2 * ATTN_WIDTH].reshape(batch, keep, 2, HEADS, HEAD_DIM)[None])

    xs =x_sample.reshape(s_rows, d)
    n1s = _norm_mod(xs, g_norm1[0], mod_s, 0, 1, s_rows)
    zs = _matmul(n1s, wts["w_in"], s_rows, 1024)
    zs3 = zs.reshape(dec_batch, dec_seq, PROJ_WIDTH)
    pad_rows = 16
    zs_pad = jnp.pad(zs3, ((0, 0), (0, pad_rows - dec_seq), (0, 0)))
    u_pad = zs_pad[:, :, :POOL_WIDTH].reshape(dec_batch * pad_rows, POOL_WIDTH)
    halo_s = jnp.pad(state_pool[0], ((0, 0), (HALO - POOL_HIST, 0), (0, 0))).reshape(dec_batch * HALO, POOL_WIDTH)
    pool_s = _pool(
        u_pad, halo_s,
        pl.BlockSpec((pad_rows, POOL_WIDTH), lambda b, i: (b, 0)),
        pl.BlockSpec((HALO, POOL_WIDTH), lambda b, i: (b, 0)),
        wts["pool_w"], pool_scale[0], (dec_batch, 1), pad_rows, False, PAST_LEN)
    pool_s = pool_s.reshape(dec_batch, pad_rows, POOL_WIDTH)[:, :dec_seq].reshape(s_rows, POOL_WIDTH)
    qkv_pad = zs_pad[:, :, QKV_OFF:GATE_OFF].reshape(dec_batch * pad_rows, GATE_OFF - QKV_OFF)
    attn_s = _attn_sample(qkv_pad, caches, slopes, dec_batch, dec_seq, pad_rows)
    attn_s = attn_s.reshape(dec_batch, pad_rows, ATTN_WIDTH)[:, :dec_seq].reshape(s_rows, ATTN_WIDTH)
    merged_s = _merge(pool_s, attn_s, wts["w_pool_proj"], wts["w_attn_proj"], zs, s_rows, 1024)
    hs = _resid_proj(merged_s, wts["w_out"], xs, mod_s, 2, s_rows, 1024)
    n2s = _norm_mod(hs, g_norm2[0], mod_s, 3, 4, s_rows)
    lane_rows = -(-s_rows // LANES) * LANES
    n2s_pad = jnp.pad(n2s, ((0, lane_rows - s_rows), (0, 0)))
    hs_pad = jnp.pad(hs, ((0, lane_rows - s_rows), (0, 0)))
    mod_s_pad = _Mod(jnp.pad(mod_s.arr, ((0, lane_rows - s_rows), (0, 0))), None)
    y_sample = _peer(n2s_pad, mod_s_pad, hs_pad, wts, g_final, lane_rows, lane_rows, lane_rows)
    y_sample = y_sample[:s_rows].reshape(dec_batch, dec_seq, d)

    new_pool_s = jnp.concatenate([state_pool[0], zs3[:, :, :POOL_WIDTH]], axis=1)[:, -POOL_HIST:][None]
    new_kv_s = []
    for g, ((window, _), cache) in enumerate(zip(ATTN_GROUPS, caches)):
        off = QKV_OFF + (g * 3 + 1) * ATTN_WIDTH
        new_rows = zs3[:, :, off:off + 2 * ATTN_WIDTH].reshape(dec_batch, dec_seq, 2, HEADS, HEAD_DIM)
        new_kv_s.append(jnp.concatenate([cache, new_rows], axis=1)[:, -window:][None])

    return (y_prompt, y_sample, new_pool_p, new_kv_p[0], new_kv_p[1], new_kv_p[2],
            new_pool_s, new_kv_s[0], new_kv_s[1], new_kv_s[2])
```

```python
import functools
import math

import jax
import jax.numpy as jnp
from jax import lax
from jax.experimental import pallas as pl
from jax.experimental.pallas import tpu as pltpu

F32 = jnp.float32
BF16 = jnp.bfloat16

D_MODEL = 2048
POOL_WINDOWS = (2, 4, 8, 16)
POOL_GROUPS = len(POOL_WINDOWS)
POOL_WIDTH = D_MODEL // 2
POOL_GROUP_WIDTH = POOL_WIDTH // POOL_GROUPS
POOL_HIST = max(POOL_WINDOWS) - 1
ATTN_GROUPS = ((128, 1), (512, 4), (2048, 16))
N_ATTN_GROUPS = len(ATTN_GROUPS)
HEADS = 8
HEAD_DIM = 128
ATTN_WIDTH = HEADS * HEAD_DIM
ATTN_CTX = 128
QKV_OFF = POOL_WIDTH
GATE_OFF = QKV_OFF + N_ATTN_GROUPS * 3 * ATTN_WIDTH
PROJ_WIDTH = GATE_OFF + 2 * D_MODEL
PEER_HEADS = 8
PEER_NKEYS = 128
PEER_TOPK = 16
NORM_EPS = 1e-6
NEG_INF = -1e30
PAST_LEN = 16384

LANES = 128
SUBLANES = 8
MXU_WIDTH = 256
HALO = 16
VMEM_LIMIT = 56 * 1024 * 1024
ATTN_TILE = 2048
ATTN_UNROLL = 4
EXPERT_BLOCK_I = 4


def _params(*sem):
    return pltpu.CompilerParams(dimension_semantics=sem, vmem_limit_bytes=VMEM_LIMIT)


def _dot(a, b):
    return jnp.dot(a, b, preferred_element_type=F32)


def _dot_nt(a, b):
    return lax.dot_general(a, b, (((1,), (1,)), ((), ())), preferred_element_type=F32)


def _ada_body(c_ref, w_ref, b_ref, o_ref):
    c = c_ref[...]
    s = c * jax.nn.sigmoid(c)
    o_ref[...] = _dot(s.astype(BF16), w_ref[...].astype(BF16)) + b_ref[...]


def _adaln(c_all, w_ada, b_ada):
    rows, d = c_all.shape
    n = w_ada.shape[1]
    tn = 1024
    return pl.pallas_call(
        _ada_body, grid=(n // tn,),
        in_specs=[pl.BlockSpec((rows, d), lambda j: (0, 0)),
                  pl.BlockSpec((d, tn), lambda j: (0, j)),
                  pl.BlockSpec((1, tn), lambda j: (0, j))],
        out_specs=pl.BlockSpec((rows, tn), lambda j: (0, j)),
        out_shape=jax.ShapeDtypeStruct((rows, n), F32),
        compiler_params=_params("parallel"), name="adaln",
    )(c_all, w_ada, b_ada.reshape(1, n))


class _Mod:
    def __init__(self, arr, rows_per_group):
        self.arr = arr
        self.rows_per_group = rows_per_group

    def spec(self, chunk, tm, tn=D_MODEL, ncols=1):
        per = D_MODEL // tn
        if self.rows_per_group is None:
            if ncols == 1:
                return pl.BlockSpec((tm, tn), lambda i: (i, chunk * per))
            return pl.BlockSpec((tm, tn), lambda i, j: (i, chunk * per + j))
        tpg = self.rows_per_group // tm
        if ncols == 1:
            return pl.BlockSpec((None, 1, tn), lambda i: (i // tpg, 0, chunk * per))
        return pl.BlockSpec((None, 1, tn), lambda i, j: (i // tpg, 0, chunk * per + j))


def _norm_mod_body(x_ref, g_ref, sc_ref, sh_ref, o_ref, *, transpose):
    x = x_ref[...]
    y = x * lax.rsqrt(jnp.mean(x * x, axis=-1, keepdims=True) + NORM_EPS) * g_ref[...]
    y = y * (1.0 + sc_ref[...]) + sh_ref[...]
    o_ref[...] = (y.T if transpose else y).astype(o_ref.dtype)


def _norm_mod(x, gain, mod, shift_chunk, scale_chunk, tm, transpose=False):
    rows, d = x.shape
    out_spec = pl.BlockSpec((d, tm), lambda i: (0, i)) if transpose else pl.BlockSpec((tm, d), lambda i: (i, 0))
    return pl.pallas_call(
        functools.partial(_norm_mod_body, transpose=transpose), grid=(rows // tm,),
        in_specs=[pl.BlockSpec((tm, d), lambda i: (i, 0)),
                  pl.BlockSpec((1, d), lambda i: (0, 0)),
                  mod.spec(scale_chunk, tm), mod.spec(shift_chunk, tm)],
        out_specs=out_spec,
        out_shape=jax.ShapeDtypeStruct((d, rows) if transpose else (rows, d), BF16),
        compiler_params=_params("parallel"), name="norm_mod",
    )(x, gain.reshape(1, d), mod.arr, mod.arr)


def _mm_body(x_ref, w_ref, o_ref):
    o_ref[...] = _dot(x_ref[...], w_ref[...]).astype(o_ref.dtype)


def _matmul(x, w, tm, tn, out_dtype=F32):
    rows, k = x.shape
    n = w.shape[1]
    return pl.pallas_call(
        _mm_body, grid=(rows // tm, n // tn),
        in_specs=[pl.BlockSpec((tm, k), lambda i, j: (i, 0)),
                  pl.BlockSpec((k, tn), lambda i, j: (0, j))],
        out_specs=pl.BlockSpec((tm, tn), lambda i, j: (i, j)),
        out_shape=jax.ShapeDtypeStruct((rows, n), out_dtype),
        compiler_params=_params("parallel", "parallel"), name="in_proj",
    )(x, w)


def _pool_body(u_ref, h_ref, pw_ref, ps_ref, o_ref, ext_ref, *, tm, zero_first, pos0):
    i = pl.program_id(1)
    halo = h_ref[...]
    if zero_first:
        halo = jnp.where(i == 0, 0.0, halo)
    ext_ref[0:HALO, :] = halo
    ext_ref[HALO:HALO + tm, :] = u_ref[...]
    pos = pos0 + i * tm + lax.broadcasted_iota(jnp.int32, (tm, 1), 0)
    for g, w in enumerate(POOL_WINDOWS):
        cols = slice(g * POOL_GROUP_WIDTH, (g + 1) * POOL_GROUP_WIDTH)
        tok = ext_ref[HALO:HALO + tm, cols]
        win = tok
        for back in range(1, w):
            win = win + ext_ref[HALO - back:HALO - back + tm, cols]
        count = jnp.minimum(w, pos + 1).astype(F32)
        pooled = win / count - tok
        mixed = _dot(pooled.astype(BF16), pw_ref[g])
        o_ref[:, cols] = (mixed * ps_ref[:, cols]).astype(o_ref.dtype)


def _pool(u_arr, halo_arr, u_spec, halo_spec, pool_w, pool_scale, grid, tm, zero_first, pos0):
    rows = grid[0] * grid[1] * tm
    body = functools.partial(_pool_body, tm=tm, zero_first=zero_first, pos0=pos0)
    return pl.pallas_call(
        body, grid=grid,
        in_specs=[u_spec, halo_spec,
                  pl.BlockSpec((POOL_GROUPS, POOL_GROUP_WIDTH, POOL_GROUP_WIDTH), lambda b, i: (0, 0, 0)),
                  pl.BlockSpec((1, POOL_WIDTH), lambda b, i: (0, 0))],
        out_specs=pl.BlockSpec((tm, POOL_WIDTH), lambda b, i: (b * grid[1] + i, 0)),
        out_shape=jax.ShapeDtypeStruct((rows, POOL_WIDTH), BF16),
        scratch_shapes=[pltpu.VMEM((HALO + tm, POOL_WIDTH), F32)],
        compiler_params=_params("parallel", "parallel"), name="pool_mix",
    )(u_arr, halo_arr, pool_w, pool_scale.reshape(1, POOL_WIDTH))


def _rows(start, dil, count=ATTN_CTX):
    return pl.ds(start, count, stride=dil) if dil > 1 else pl.ds(start, count)


def _attn_prompt_body(sl_ref, *refs):
    n_in = 5 * N_ATTN_GROUPS
    ins, o_ref, kv_out, scr = refs[:n_in], refs[n_in], refs[n_in + 1:n_in + 1 + 2 * N_ATTN_GROUPS], \
        refs[n_in + 1 + 2 * N_ATTN_GROUPS:]
    acc_ref, m_ref, l_ref = scr[2 * N_ATTN_GROUPS:]
    head = pl.program_id(1)
    t = pl.program_id(2)
    key_col = lax.broadcasted_iota(jnp.int32, (ATTN_CTX, 2 * ATTN_CTX), 1)
    steps = (ATTN_CTX + lax.broadcasted_iota(jnp.int32, (ATTN_CTX, 2 * ATTN_CTX), 0) - key_col).astype(F32)
    in_band = jnp.logical_and(steps >= 0.0, steps <= float(ATTN_CTX))
    ones = jnp.ones((2 * ATTN_CTX, LANES), BF16)
    scale = 1.0 / math.sqrt(HEAD_DIM)
    for g, (window, dil) in enumerate(ATTN_GROUPS):
        q_ref, k_ref, v_ref, kp_ref, vp_ref = ins[5 * g:5 * g + 5]
        kx_ref, vx_ref = scr[2 * g], scr[2 * g + 1]
        kx_ref[0:window, :] = kp_ref[...]
        vx_ref[0:window, :] = vp_ref[...]
        kx_ref[window:window + ATTN_TILE, :] = k_ref[...]
        vx_ref[window:window + ATTN_TILE, :] = v_ref[...]
        kv_out[2 * g][...] = k_ref[ATTN_TILE - window:, :]
        kv_out[2 * g + 1][...] = v_ref[ATTN_TILE - window:, :]
        bias = (sl_ref[g, head] * float(dil)) * steps

        def attend(idx, window=window, dil=dil, q_ref=q_ref, kx_ref=kx_ref, vx_ref=vx_ref, bias=bias):
            n = idx // dil
            r = idx % dil
            start = n * window + r
            q = q_ref[_rows(start, dil), :].astype(BF16)
            k2 = kx_ref[_rows(start, dil, 2 * ATTN_CTX), :].astype(BF16)
            v2 = vx_ref[_rows(start, dil, 2 * ATTN_CTX), :].astype(BF16)
            first_key = jnp.where(jnp.logical_or(t > 0, n > 0), 0, ATTN_CTX)
            valid = jnp.logical_and(in_band, key_col >= first_key)
            s = jnp.where(valid, _dot_nt(q, k2) * scale - bias, NEG_INF)
            m = jnp.max(s, axis=1, keepdims=True)
            p = jnp.exp(s - m).astype(BF16)
            both = _dot(p, jnp.concatenate([v2, ones], axis=1))
            m = jnp.broadcast_to(m, (ATTN_CTX, LANES))
            return _rows(start, dil), both[:, :HEAD_DIM], m, both[:, HEAD_DIM:]

        def merge(rows, acc, m, l, g=g):
            if g > 0:
                m_old = m_ref[rows, :]
                m_new = jnp.maximum(m_old, m)
                a_old = jnp.exp(m_old - m_new)
                a_new = jnp.exp(m - m_new)
                acc = acc_ref[rows, :] * a_old + acc * a_new
                l = l_ref[rows, :] * a_old + l * a_new
                m = m_new
            acc_ref[rows, :] = acc
            m_ref[rows, :] = m
            l_ref[rows, :] = l

        def blocks(it, carry, attend=attend, merge=merge):
            done = [attend(it * ATTN_UNROLL + u) for u in range(ATTN_UNROLL)]
            for part in done:
                merge(*part)
            return carry

        lax.fori_loop(0, ATTN_TILE // ATTN_CTX // ATTN_UNROLL, blocks, 0)
    o_ref[...] = (acc_ref[...] / l_ref[...]).astype(o_ref.dtype)


def _attn_prompt(z, slopes, batch, seq):
    tiles = seq // ATTN_TILE
    col0 = QKV_OFF // HEAD_DIM
    in_specs = [pl.BlockSpec(memory_space=pltpu.SMEM)]
    operands = [slopes]
    scratch = []
    for g, (window, _) in enumerate(ATTN_GROUPS):
        per_tile = ATTN_TILE // window
        for part in range(3):
            col = col0 + (g * 3 + part) * HEADS
            in_specs.append(pl.BlockSpec((ATTN_TILE, HEAD_DIM),
                                         lambda b, h, t, col=col: (b * tiles + t, col + h)))
            operands.append(z)
        for part in (1, 2):
            col = col0 + (g * 3 + part) * HEADS
            in_specs.append(pl.BlockSpec(
                (window, HEAD_DIM),
                lambda b, h, t, col=col, per_tile=per_tile:
                (jnp.maximum((b * tiles + t) * per_tile - 1, 0), col + h)))
            operands.append(z)
        scratch += [pltpu.VMEM((window + ATTN_TILE, HEAD_DIM), F32)] * 2
    scratch += [pltpu.VMEM((ATTN_TILE, LANES), F32)] * 3
    out_specs = [pl.BlockSpec((ATTN_TILE, HEAD_DIM), lambda b, h, t: (b * tiles + t, h))]
    out_shape = [jax.ShapeDtypeStruct((batch * seq, ATTN_WIDTH), BF16)]
    for window, _ in ATTN_GROUPS:
        for _ in range(2):
            out_specs.append(pl.BlockSpec((window, HEAD_DIM), lambda b, h, t: (b, h)))
            out_shape.append(jax.ShapeDtypeStruct((batch * window, ATTN_WIDTH), F32))
    outs = pl.pallas_call(
        _attn_prompt_body, grid=(batch, HEADS, tiles),
        in_specs=in_specs, out_specs=out_specs, out_shape=out_shape,
        scratch_shapes=scratch,
        compiler_params=_params("parallel", "parallel", "arbitrary"), name="attn_prompt",
    )(*operands)
    return outs[0], [(outs[1 + 2 * g], outs[2 + 2 * g]) for g in range(N_ATTN_GROUPS)]


def _attn_sample_body(sl_ref, qkv_ref, c0_ref, c1_ref, c2_ref, o_ref, *, dec_seq, pad_rows):
    caches = (c0_ref, c1_ref, c2_ref)
    scale = 1.0 / math.sqrt(HEAD_DIM)
    qi_new = lax.broadcasted_iota(jnp.int32, (pad_rows, pad_rows), 0)
    kj_new = lax.broadcasted_iota(jnp.int32, (pad_rows, pad_rows), 1)
    row_kv = 2 * ATTN_WIDTH
    for h in range(HEADS):
        acc = m_run = l_run = None
        for g, (window, dil) in enumerate(ATTN_GROUPS):
            slope = sl_ref[g, h]
            base = (g * 3) * ATTN_WIDTH + h * HEAD_DIM
            q = qkv_ref[:, base:base + HEAD_DIM].astype(BF16)
            k_new = qkv_ref[:, base + ATTN_WIDTH:base + ATTN_WIDTH + HEAD_DIM].astype(BF16)
            v_new = qkv_ref[:, base + 2 * ATTN_WIDTH:base + 2 * ATTN_WIDTH + HEAD_DIM].astype(BF16)
            n_res = min(dil, dec_seq)
            c_ref = caches[g]
            k_c = jnp.concatenate([c_ref[:, c * row_kv + h * HEAD_DIM:c * row_kv + (h + 1) * HEAD_DIM]
                                   for c in range(n_res)], axis=0).astype(BF16)
            v_c = jnp.concatenate([c_ref[:, c * row_kv + ATTN_WIDTH + h * HEAD_DIM:
                                         c * row_kv + ATTN_WIDTH + (h + 1) * HEAD_DIM]
                                   for c in range(n_res)], axis=0).astype(BF16)
            n_keys = n_res * ATTN_CTX
            qi = lax.broadcasted_iota(jnp.int32, (pad_rows, n_keys), 0)
            kj = lax.broadcasted_iota(jnp.int32, (pad_rows, n_keys), 1)
            key_pos = (kj & (ATTN_CTX - 1)) * dil + (kj >> 7)
            dist = window + qi - key_pos
            ok_c = ((dist & (dil - 1)) == 0) & (dist <= window) & (qi < dec_seq)
            s_c = jnp.where(ok_c, _dot_nt(q, k_c) * scale - slope * dist.astype(F32), NEG_INF)
            dist_n = qi_new - kj_new
            ok_n = (dist_n >= 0) & ((dist_n & (dil - 1)) == 0) & (kj_new < dec_seq)
            s_n = jnp.where(ok_n, _dot_nt(q, k_new) * scale - slope * dist_n.astype(F32), NEG_INF)
            m = jnp.maximum(jnp.max(s_c, axis=1, keepdims=True), jnp.max(s_n, axis=1, keepdims=True))
            p_c = jnp.exp(s_c - m)
            p_n = jnp.exp(s_n - m)
            l = jnp.sum(p_c, axis=1, keepdims=True) + jnp.sum(p_n, axis=1, keepdims=True)
            a = _dot(p_c.astype(BF16), v_c) + _dot(p_n.astype(BF16), v_new)
            if g == 0:
                acc, m_run, l_run = a, m, l
            else:
                m_new = jnp.maximum(m_run, m)
                a_old = jnp.exp(m_run - m_new)
                a_cur = jnp.exp(m - m_new)
                acc = acc * a_old + a * a_cur
                l_run = l_run * a_old + l * a_cur
                m_run = m_new
        o_ref[:, h * HEAD_DIM:(h + 1) * HEAD_DIM] = (acc / l_run).astype(o_ref.dtype)


def _attn_sample(qkv_pad, caches, slopes, dec_batch, dec_seq, pad_rows):
    in_specs = [pl.BlockSpec(memory_space=pltpu.SMEM),
                pl.BlockSpec((pad_rows, qkv_pad.shape[1]), lambda n: (n, 0))]
    operands = [slopes, qkv_pad]
    for (window, dil), cache in zip(ATTN_GROUPS, caches):
        n_res = min(dil, dec_seq)
        view = cache.reshape(dec_batch, window // dil, dil * 2 * ATTN_WIDTH)
        in_specs.append(pl.BlockSpec((None, window // dil, n_res * 2 * ATTN_WIDTH), lambda n: (n, 0, 0)))
        operands.append(view)
    body = functools.partial(_attn_sample_body, dec_seq=dec_seq, pad_rows=pad_rows)
    return pl.pallas_call(
        body, grid=(dec_batch,),
        in_specs=in_specs,
        out_specs=pl.BlockSpec((pad_rows, ATTN_WIDTH), lambda n: (n, 0)),
        out_shape=jax.ShapeDtypeStruct((dec_batch * pad_rows, ATTN_WIDTH), BF16),
        compiler_params=_params("parallel"), name="attn_sample",
    )(*operands)


def _merge_body(p_ref, a_ref, wp_ref, wa_ref, gp_ref, ga_ref, o_ref):
    yp = _dot(p_ref[...], wp_ref[...])
    ya = _dot(a_ref[...], wa_ref[...])
    o_ref[...] = (jax.nn.sigmoid(gp_ref[...]) * yp + jax.nn.sigmoid(ga_ref[...]) * ya).astype(o_ref.dtype)


def _merge(pool_out, attn_out, w_pool_proj, w_attn_proj, z, tm, tn):
    rows = pool_out.shape[0]
    gp0 = GATE_OFF // tn
    ga0 = (GATE_OFF + D_MODEL) // tn
    return pl.pallas_call(
        _merge_body, grid=(rows // tm, D_MODEL // tn),
        in_specs=[pl.BlockSpec((tm, POOL_WIDTH), lambda i, j: (i, 0)),
                  pl.BlockSpec((tm, ATTN_WIDTH), lambda i, j: (i, 0)),
                  pl.BlockSpec((POOL_WIDTH, tn), lambda i, j: (0, j)),
                  pl.BlockSpec((ATTN_WIDTH, tn), lambda i, j: (0, j)),
                  pl.BlockSpec((tm, tn), lambda i, j: (i, gp0 + j)),
                  pl.BlockSpec((tm, tn), lambda i, j: (i, ga0 + j))],
        out_specs=pl.BlockSpec((tm, tn), lambda i, j: (i, j)),
        out_shape=jax.ShapeDtypeStruct((rows, D_MODEL), BF16),
        compiler_params=_params("parallel", "parallel"), name="gated_merge",
    )(pool_out, attn_out, w_pool_proj, w_attn_proj, z, z)


def _resid_body(m_ref, w_ref, x_ref, ga_ref, o_ref):
    o_ref[...] = x_ref[...] + ga_ref[...] * _dot(m_ref[...], w_ref[...])


def _resid_proj(merged, w_out, x, mod, gate_chunk, tm, tn):
    rows = merged.shape[0]
    return pl.pallas_call(
        _resid_body, grid=(rows // tm, D_MODEL // tn),
        in_specs=[pl.BlockSpec((tm, D_MODEL), lambda i, j: (i, 0)),
                  pl.BlockSpec((D_MODEL, tn), lambda i, j: (0, j)),
                  pl.BlockSpec((tm, tn), lambda i, j: (i, j)),
                  mod.spec(gate_chunk, tm, tn, ncols=2)],
        out_specs=pl.BlockSpec((tm, tn), lambda i, j: (i, j)),
        out_shape=jax.ShapeDtypeStruct((rows, D_MODEL), F32),
        compiler_params=_params("parallel", "parallel"), name="resid_proj",
    )(merged, w_out, x, mod.arr)


def _odd_even_merge_sort(n):
    pairs = []
    p = 1
    while p < n:
        k = p
        while k >= 1:
            for j in range(k % p, n - k, 2 * k):
                for i in range(min(k, n - j - k)):
                    if (i + j) // (2 * p) == (i + j + k) // (2 * p):
                        pairs.append((i + j, i + j + k))
            k //= 2
        p *= 2
    return pairs


def _top16(s):
    k = PEER_TOPK
    x = [s[SUBLANES * i:SUBLANES * (i + 1), :] for i in range(PEER_NKEYS // SUBLANES)]
    for i, j in _odd_even_merge_sort(k):
        x[i], x[j] = jnp.maximum(x[i], x[j]), jnp.minimum(x[i], x[j])
    dropped = jnp.full_like(x[0], NEG_INF)
    shift = SUBLANES // 2
    while shift >= 1:
        y = [pltpu.roll(v, shift, axis=0) for v in x]
        dropped = jnp.maximum(dropped, pltpu.roll(dropped, shift, axis=0))
        lo = [jnp.minimum(x[i], y[k - 1 - i]) for i in range(k)]
        dropped = functools.reduce(jnp.maximum, lo, dropped)
        x = [jnp.maximum(x[i], y[k - 1 - i]) for i in range(k)]
        stride = k // 2
        while stride >= 1:
            for i in range(k):
                if (i // stride) % 2 == 0:
                    j = i + stride
                    x[i], x[j] = jnp.maximum(x[i], x[j]), jnp.minimum(x[i], x[j])
            stride //= 2
        shift //= 2
    return x, dropped


def _gate_rows(ref, h, row):
    r = ref[h, row:row + 1, :]
    return jnp.broadcast_to(r, (2 * SUBLANES, r.shape[1])).astype(BF16)


def _route_body(xT_ref, wq_ref, sk_ref, n_ref, c_ref, rank_ref, e1_ref, q_ref):
    q_ref[...] = _dot(wq_ref[...], xT_ref[...])
    k = PEER_TOPK

    def head(h, carry):
        half = []
        for p in range(2):
            row0 = pl.multiple_of((h * 2 + p) * PEER_NKEYS, PEER_NKEYS)
            q = q_ref[pl.ds(row0, PEER_NKEYS), :].astype(BF16)
            half.append(_dot(sk_ref[h * 2 + p], q))
        s0, s1 = half
        v0, rest0 = _top16(s0)
        v1, rest1 = _top16(s1)
        row = lax.broadcasted_iota(jnp.int32, v0[0].shape, 0)

        def along_sublanes(vals):
            return functools.reduce(lambda acc, b: jnp.where(row == b, vals[b], acc), range(1, SUBLANES), vals[0])

        v1_lo, v1_hi, v0_hi = along_sublanes(v1[:SUBLANES]), along_sublanes(v1[SUBLANES:]), along_sublanes(v0[SUBLANES:])
        cand = [v0[0] + v1_lo, v0[0] + v1_hi, v0_hi + v1[0]]
        for a in range(1, SUBLANES):
            cand.append(jnp.where(row <= (k + 1) // (a + 1) - 1, v0[a] + v1_lo, NEG_INF))
        top = (v0[0] + v1[0])[0:1, :]
        taken = jnp.zeros_like(top)
        tau = jnp.full_like(top, NEG_INF)
        below = jnp.full_like(top, NEG_INF)
        z = jnp.zeros_like(top)
        for _ in range(k + 1):
            mx = jnp.max(functools.reduce(jnp.maximum, cand), axis=0, keepdims=True)
            eqs = [ca == mx for ca in cand]
            cnt = jnp.sum(functools.reduce(jnp.add, [jnp.where(eq, 1.0, 0.0) for eq in eqs]), axis=0, keepdims=True)
            cand = [jnp.where(eq, NEG_INF, ca) for eq, ca in zip(eqs, cand)]
            need = taken < float(k)
            first_below = jnp.logical_and(jnp.logical_not(need), below <= 0.5 * NEG_INF)
            tau = jnp.where(need, mx, tau)
            z = z + jnp.where(need, cnt * jnp.exp(mx - top), 0.0)
            below = jnp.where(first_below, mx, below)
            taken = taken + jnp.where(need, cnt, 0.0)
        below = jnp.maximum(below, jnp.maximum(rest0 + v1[0], v0[0] + rest1)[0:1, :])
        cut = 0.5 * (tau + below)
        rank1 = jnp.zeros_like(s1)
        n_pass = jnp.zeros_like(s0)
        for b in range(k):
            v1b = v1[b][0:1, :]
            rank1 = rank1 + jnp.where(s1 < v1b, 1.0, 0.0)
            n_pass = n_pass + jnp.where(s0 + v1b >= cut, 1.0, 0.0)
        n_ref[h] = n_pass
        c_ref[h] = jnp.exp(s0 - v0[0][0:1, :]) / z
        rank_ref[h] = rank1.astype(rank_ref.dtype)
        e1_ref[h] = jnp.exp(s1 - v1[0][0:1, :]).astype(e1_ref.dtype)
        return carry

    lax.fori_loop(0, PEER_HEADS, head, 0)


def _route(xT, w_pqT, sub_keys16, tm):
    d, tokens = xT.shape
    nq = w_pqT.shape[0]
    shape = (PEER_HEADS, PEER_NKEYS, tokens)
    ospec = pl.BlockSpec((PEER_HEADS, PEER_NKEYS, tm), lambda i: (0, 0, i))
    return pl.pallas_call(
        _route_body, grid=(tokens // tm,),
        in_specs=[pl.BlockSpec((d, tm), lambda i: (0, i)),
                  pl.BlockSpec((nq, d), lambda i: (0, 0)),
                  pl.BlockSpec((2 * PEER_HEADS, PEER_NKEYS, sub_keys16.shape[2]), lambda i: (0, 0, 0))],
        out_specs=[ospec] * 4,
        out_shape=[jax.ShapeDtypeStruct(shape, F32), jax.ShapeDtypeStruct(shape, F32),
                   jax.ShapeDtypeStruct(shape, BF16), jax.ShapeDtypeStruct(shape, BF16)],
        scratch_shapes=[pltpu.VMEM((nq, tm), F32)],
        compiler_params=_params("parallel"), name="peer_route",
    )(xT, w_pqT, sub_keys16)


def _peer_body(xT_ref, u_ref, vT_ref, na_ref, ca_ref, nb_ref, cb_ref, rank_ref, e1_ref, o_ref, act_ref, gated_ref):
    s = pl.program_id(1)
    eb = EXPERT_BLOCK_I * PEER_NKEYS

    @pl.when(s == 0)
    def _():
        o_ref[...] = jnp.zeros_like(o_ref)
        act_ref[...] = jnp.zeros_like(act_ref)
        gated_ref[...] = jnp.zeros_like(gated_ref)

    def stage_a(rows):
        return jax.nn.gelu(_dot(u_ref[rows, :], xT_ref[...])).astype(BF16)

    chunk = 16
    n_chunks = PEER_NKEYS // chunk

    def stage_b(act, n_ref, c_ref, row0):
        parts = []
        for ii in range(EXPERT_BLOCK_I):
            w = [None] * n_chunks
            for h in range(PEER_HEADS):
                n_rows = _gate_rows(n_ref, h, row0 + ii)
                c_rows = _gate_rows(c_ref, h, row0 + ii)
                for jc in range(n_chunks):
                    keys = slice(jc * chunk, (jc + 1) * chunk)
                    sel = jnp.where(rank_ref[h, keys, :] < n_rows, e1_ref[h, keys, :], 0.0) * c_rows
                    w[jc] = sel if w[jc] is None else w[jc] + sel
            base = ii * PEER_NKEYS
            parts += [act[base + jc * chunk:base + (jc + 1) * chunk, :] * w[jc] for jc in range(n_chunks)]
        return jnp.concatenate(parts, axis=0)

    read, write = (s + 1) % 2, s % 2
    act0 = stage_a(slice(0, eb))
    gated1 = stage_b(act_ref[read], na_ref, ca_ref, EXPERT_BLOCK_I)
    act1 = stage_a(slice(eb, 2 * eb))
    gated0 = stage_b(act0, nb_ref, cb_ref, 0)
    o_ref[...] += _dot(vT_ref[...], jnp.concatenate([gated_ref[read], gated1], axis=0))
    act_ref[write] = act1
    gated_ref[write] = gated0


def _peer_mix(xT, peer_u, peer_vT, n_pass, c, rank1, e1, tm):
    d, tokens = xT.shape
    eb = EXPERT_BLOCK_I * PEER_NKEYS
    steps = peer_u.shape[0] // (2 * eb)
    prev_rows = pl.BlockSpec((PEER_HEADS, 2 * EXPERT_BLOCK_I, tm), lambda i, s: (0, jnp.maximum(s - 1, 0), i))
    cur_rows = pl.BlockSpec((PEER_HEADS, 2 * EXPERT_BLOCK_I, tm), lambda i, s: (0, jnp.minimum(s, steps - 1), i))
    col_spec = pl.BlockSpec((PEER_HEADS, PEER_NKEYS, tm), lambda i, s: (0, 0, i))
    return pl.pallas_call(
        _peer_body, grid=(tokens // tm, steps + 1),
        in_specs=[pl.BlockSpec((d, tm), lambda i, s: (0, i)),
                  pl.BlockSpec((2 * eb, d), lambda i, s: (jnp.minimum(s, steps - 1), 0)),
                  pl.BlockSpec((d, 2 * eb), lambda i, s: (0, jnp.maximum(s - 1, 0))),
                  prev_rows, prev_rows, cur_rows, cur_rows, col_spec, col_spec],
        out_specs=pl.BlockSpec((d, tm), lambda i, s: (0, i)),
        out_shape=jax.ShapeDtypeStruct((d, tokens), F32),
        scratch_shapes=[pltpu.VMEM((2, eb, tm), BF16), pltpu.VMEM((2, eb, tm), BF16)],
        compiler_params=_params("parallel", "arbitrary"), name="peer_mix",
    )(xT, peer_u, peer_vT, n_pass, c, n_pass, c, rank1, e1)


def _final_body(h_ref, pT_ref, ga_ref, g_ref, o_ref):
    h = h_ref[...] + ga_ref[...] * pT_ref[...].T
    o_ref[...] = h * lax.rsqrt(jnp.mean(h * h, axis=-1, keepdims=True) + NORM_EPS) * g_ref[...]


def _final(h, peerT, mod, gate_chunk, g_final, tm):
    rows, d = h.shape
    return pl.pallas_call(
        _final_body, grid=(rows // tm,),
        in_specs=[pl.BlockSpec((tm, d), lambda i: (i, 0)),
                  pl.BlockSpec((d, tm), lambda i: (0, i)),
                  mod.spec(gate_chunk, tm),
                  pl.BlockSpec((1, d), lambda i: (0, 0))],
        out_specs=pl.BlockSpec((tm, d), lambda i: (i, 0)),
        out_shape=jax.ShapeDtypeStruct((rows, d), F32),
        compiler_params=_params("parallel"), name="final_norm",
    )(h, peerT, mod.arr, g_final.reshape(1, d))


def _alibi_slopes():
    idx = jnp.arange(N_ATTN_GROUPS * HEADS, dtype=F32) + 1.0
    return jnp.exp2(-8.0 * idx / (N_ATTN_GROUPS * HEADS)).reshape(N_ATTN_GROUPS, HEADS)


def _peer(xT, mod, h, wts, g_final, tm_route, tm_mix, tm_final):
    n_pass, c, rank1, e1 = _route(xT, wts["w_pqT"], wts["sub_keys"], tm_route)
    peerT = _peer_mix(xT, wts["peer_u"], wts["peer_vT"], n_pass, c, rank1, e1, tm_mix)
    return _final(h, peerT, mod, 5, g_final, tm_final)


def kernel(x_prompt, x_sample, state_pool, cache_kv0, cache_kv1, cache_kv2, c_prompt, c_sample, w_ada, b_ada,
           g_norm1, g_norm2, w_in, pool_w, pool_scale, w_pool_proj, w_attn_proj, w_out, w_pq, sub_keys, peer_u,
           peer_v, g_final):
    batch, seq, d = x_prompt.shape
    dec_batch, dec_seq, _ = x_sample.shape
    depth = w_ada.shape[0]
    assert depth == 1 and d == D_MODEL and seq % ATTN_TILE == 0 and dec_seq <= 8
    caches = (cache_kv0[0], cache_kv1[0], cache_kv2[0])
    for (window, _), cache in zip(ATTN_GROUPS, caches):
        assert cache.shape[1] == window

    wts = {
        "w_in": w_in[0].astype(BF16),
        "pool_w": pool_w[0].astype(BF16),
        "w_pool_proj": w_pool_proj[0].astype(BF16),
        "w_attn_proj": w_attn_proj[0].astype(BF16),
        "w_out": w_out[0].astype(BF16),
        "w_pqT": w_pq[0].T.astype(BF16),
        "sub_keys": sub_keys[0].reshape(2 * PEER_HEADS, PEER_NKEYS, -1).astype(BF16),
        "peer_u": peer_u[0].astype(BF16),
        "peer_vT": peer_v[0].T.astype(BF16),
    }
    slopes = _alibi_slopes()

    n_cond = batch + dec_batch
    cond_rows = -(-n_cond // 8) * 8
    c_all = jnp.concatenate([c_prompt, c_sample, jnp.zeros((cond_rows - n_cond, d), F32)], axis=0)
    mod_all = _adaln(c_all, w_ada[0], b_ada[0])
    mod_p = _Mod(mod_all[:batch].reshape(batch, 1, 6 * d), seq)
    s_rows = dec_batch * dec_seq
    mod_s = _Mod(jnp.repeat(mod_all[batch:n_cond], dec_seq, axis=0), None)

    xp = x_prompt.reshape(batch * seq, d)
    n1 = _norm_mod(xp, g_norm1[0], mod_p, 0, 1, 512)
    z = _matmul(n1, wts["w_in"], 1024, 1024)
    tm_pool = 256
    nt = seq // tm_pool
    pool_out = _pool(
        z, z,
        pl.BlockSpec((tm_pool, POOL_WIDTH), lambda b, i: (b * nt + i, 0)),
        pl.BlockSpec((HALO, POOL_WIDTH), lambda b, i: (jnp.maximum((b * nt + i) * (tm_pool // HALO) - 1, 0), 0)),
        wts["pool_w"], pool_scale[0], (batch, nt), tm_pool, True, 0)
    attn_out, kv_rows = _attn_prompt(z, slopes, batch, seq)
    merged = _merge(pool_out, attn_out, wts["w_pool_proj"], wts["w_attn_proj"], z, 512, 1024)
    hp = _resid_proj(merged, wts["w_out"], xp, mod_p, 2, 512, 1024)
    n2T = _norm_mod(hp, g_norm2[0], mod_p, 3, 4, 512, transpose=True)
    y_prompt = _peer(n2T, mod_p, hp, wts, g_final, 256, 512, 256).reshape(batch, seq, d)

    new_pool_p = z.reshape(batch, seq, PROJ_WIDTH)[:, seq - POOL_HIST:, :POOL_WIDTH][None]
    new_kv_p = [jnp.stack([k.reshape(batch, -1, HEADS, HEAD_DIM), v.reshape(batch, -1, HEADS, HEAD_DIM)], axis=2)[None]
                for k, v in kv_rows]

    xs =x_sample.reshape(s_rows, d)
    n1s = _norm_mod(xs, g_norm1[0], mod_s, 0, 1, s_rows)
    zs = _matmul(n1s, wts["w_in"], s_rows, 1024)
    zs3 = zs.reshape(dec_batch, dec_seq, PROJ_WIDTH)
    pad_rows = 16
    zs_pad = jnp.pad(zs3, ((0, 0), (0, pad_rows - dec_seq), (0, 0)))
    u_pad = zs_pad[:, :, :POOL_WIDTH].reshape(dec_batch * pad_rows, POOL_WIDTH)
    halo_s = jnp.pad(state_pool[0], ((0, 0), (HALO - POOL_HIST, 0), (0, 0))).reshape(dec_batch * HALO, POOL_WIDTH)
    pool_s = _pool(
        u_pad, halo_s,
        pl.BlockSpec((pad_rows, POOL_WIDTH), lambda b, i: (b, 0)),
        pl.BlockSpec((HALO, POOL_WIDTH), lambda b, i: (b, 0)),
        wts["pool_w"], pool_scale[0], (dec_batch, 1), pad_rows, False, PAST_LEN)
    pool_s = pool_s.reshape(dec_batch, pad_rows, POOL_WIDTH)[:, :dec_seq].reshape(s_rows, POOL_WIDTH)
    qkv_pad = zs_pad[:, :, QKV_OFF:GATE_OFF].reshape(dec_batch * pad_rows, GATE_OFF - QKV_OFF)
    attn_s = _attn_sample(qkv_pad, caches, slopes, dec_batch, dec_seq, pad_rows)
    attn_s = attn_s.reshape(dec_batch, pad_rows, ATTN_WIDTH)[:, :dec_seq].reshape(s_rows, ATTN_WIDTH)
    merged_s = _merge(pool_s, attn_s, wts["w_pool_proj"], wts["w_attn_proj"], zs, s_rows, 1024)
    hs = _resid_proj(merged_s, wts["w_out"], xs, mod_s, 2, s_rows, 1024)
    n2s = _norm_mod(hs, g_norm2[0], mod_s, 3, 4, s_rows)
    lane_rows = -(-s_rows // LANES) * LANES
    n2s_pad = jnp.pad(n2s, ((0, lane_rows - s_rows), (0, 0)))
    hs_pad = jnp.pad(hs, ((0, lane_rows - s_rows), (0, 0)))
    mod_s_pad = _Mod(jnp.pad(mod_s.arr, ((0, lane_rows - s_rows), (0, 0))), None)
    y_sample = _peer(n2s_pad.T, mod_s_pad, hs_pad, wts, g_final, lane_rows, lane_rows, lane_rows)
    y_sample = y_sample[:s_rows].reshape(dec_batch, dec_seq, d)

    new_pool_s = jnp.concatenate([state_pool[0], zs3[:, :, :POOL_WIDTH]], axis=1)[:, -POOL_HIST:][None]
    new_kv_s = []
    for g, ((window, _), cache) in enumerate(zip(ATTN_GROUPS, caches)):
        off = QKV_OFF + (g * 3 + 1) * ATTN_WIDTH
        new_rows = zs3[:, :, off:off + 2 * ATTN_WIDTH].reshape(dec_batch, dec_seq, 2, HEADS, HEAD_DIM)
        new_kv_s.append(jnp.concatenate([cache, new_rows], axis=1)[:, -window:][None])

    return (y_prompt, y_sample, new_pool_p, new_kv_p[0], new_kv_p[1], new_kv_p[2],
            new_pool_s, new_kv_s[0], new_kv_s[1], new_kv_s[2])
```

```python
import functools
import math

import jax
import jax.numpy as jnp
from jax import lax
from jax.experimental import pallas as pl
from jax.experimental.pallas import tpu as pltpu

F32 = jnp.float32
BF16 = jnp.bfloat16

D_MODEL = 2048
POOL_WINDOWS = (2, 4, 8, 16)
POOL_GROUPS = len(POOL_WINDOWS)
POOL_WIDTH = D_MODEL // 2
POOL_GROUP_WIDTH = POOL_WIDTH // POOL_GROUPS
POOL_HIST = max(POOL_WINDOWS) - 1
ATTN_GROUPS = ((128, 1), (512, 4), (2048, 16))
N_ATTN_GROUPS = len(ATTN_GROUPS)
HEADS = 8
HEAD_DIM = 128
ATTN_WIDTH = HEADS * HEAD_DIM
ATTN_CTX = 128
QKV_OFF = POOL_WIDTH
GATE_OFF = QKV_OFF + N_ATTN_GROUPS * 3 * ATTN_WIDTH
PROJ_WIDTH = GATE_OFF + 2 * D_MODEL
PEER_HEADS = 8
PEER_NKEYS = 128
PEER_TOPK = 16
NORM_EPS = 1e-6
NEG_INF = -1e30
PAST_LEN = 16384

LANES = 128
SUBLANES = 8
MXU_WIDTH = 256
HALO = 16
VMEM_LIMIT = 56 * 1024 * 1024
ATTN_TILE = 2048
ATTN_UNROLL = 8
ATTN_FOLD = 4
EXPERT_BLOCK_I = 4


def _params(*sem):
    return pltpu.CompilerParams(dimension_semantics=sem, vmem_limit_bytes=VMEM_LIMIT)


def _dot(a, b):
    return jnp.dot(a, b, preferred_element_type=F32)


def _dot_nt(a, b):
    return lax.dot_general(a, b, (((1,), (1,)), ((), ())), preferred_element_type=F32)


def _ada_body(c_ref, w_ref, b_ref, o_ref):
    c = c_ref[...]
    s = c * jax.nn.sigmoid(c)
    o_ref[...] = _dot(s.astype(BF16), w_ref[...].astype(BF16)) + b_ref[...]


def _adaln(c_all, w_ada, b_ada):
    rows, d = c_all.shape
    n = w_ada.shape[1]
    tn = 1024
    return pl.pallas_call(
        _ada_body, grid=(n // tn,),
        in_specs=[pl.BlockSpec((rows, d), lambda j: (0, 0)),
                  pl.BlockSpec((d, tn), lambda j: (0, j)),
                  pl.BlockSpec((1, tn), lambda j: (0, j))],
        out_specs=pl.BlockSpec((rows, tn), lambda j: (0, j)),
        out_shape=jax.ShapeDtypeStruct((rows, n), F32),
        compiler_params=_params("parallel"), name="adaln",
    )(c_all, w_ada, b_ada.reshape(1, n))


class _Mod:
    def __init__(self, arr, rows_per_group):
        self.arr = arr
        self.rows_per_group = rows_per_group

    def spec(self, chunk, tm, tn=D_MODEL, ncols=1):
        per = D_MODEL // tn
        if self.rows_per_group is None:
            if ncols == 1:
                return pl.BlockSpec((tm, tn), lambda i: (i, chunk * per))
            return pl.BlockSpec((tm, tn), lambda i, j: (i, chunk * per + j))
        tpg = self.rows_per_group // tm
        if ncols == 1:
            return pl.BlockSpec((None, 1, tn), lambda i: (i // tpg, 0, chunk * per))
        return pl.BlockSpec((None, 1, tn), lambda i, j: (i // tpg, 0, chunk * per + j))


def _norm_mod_body(x_ref, g_ref, sc_ref, sh_ref, o_ref, *, transpose):
    x = x_ref[...]
    y = x * lax.rsqrt(jnp.mean(x * x, axis=-1, keepdims=True) + NORM_EPS) * g_ref[...]
    y = y * (1.0 + sc_ref[...]) + sh_ref[...]
    o_ref[...] = (y.T if transpose else y).astype(o_ref.dtype)


def _norm_mod(x, gain, mod, shift_chunk, scale_chunk, tm, transpose=False):
    rows, d = x.shape
    out_spec = pl.BlockSpec((d, tm), lambda i: (0, i)) if transpose else pl.BlockSpec((tm, d), lambda i: (i, 0))
    return pl.pallas_call(
        functools.partial(_norm_mod_body, transpose=transpose), grid=(rows // tm,),
        in_specs=[pl.BlockSpec((tm, d), lambda i: (i, 0)),
                  pl.BlockSpec((1, d), lambda i: (0, 0)),
                  mod.spec(scale_chunk, tm), mod.spec(shift_chunk, tm)],
        out_specs=out_spec,
        out_shape=jax.ShapeDtypeStruct((d, rows) if transpose else (rows, d), BF16),
        compiler_params=_params("parallel"), name="norm_mod",
    )(x, gain.reshape(1, d), mod.arr, mod.arr)


def _mm_body(x_ref, w_ref, o_ref):
    o_ref[...] = _dot(x_ref[...], w_ref[...]).astype(o_ref.dtype)


def _matmul(x, w, tm, tn, out_dtype=F32):
    rows, k = x.shape
    n = w.shape[1]
    return pl.pallas_call(
        _mm_body, grid=(rows // tm, n // tn),
        in_specs=[pl.BlockSpec((tm, k), lambda i, j: (i, 0)),
                  pl.BlockSpec((k, tn), lambda i, j: (0, j))],
        out_specs=pl.BlockSpec((tm, tn), lambda i, j: (i, j)),
        out_shape=jax.ShapeDtypeStruct((rows, n), out_dtype),
        compiler_params=_params("parallel", "parallel"), name="in_proj",
    )(x, w)


def _pool_body(u_ref, h_ref, pw_ref, ps_ref, o_ref, ext_ref, *, tm, zero_first, pos0):
    i = pl.program_id(1)
    halo = h_ref[...]
    if zero_first:
        halo = jnp.where(i == 0, 0.0, halo)
    ext_ref[0:HALO, :] = halo
    ext_ref[HALO:HALO + tm, :] = u_ref[...]
    pos = pos0 + i * tm + lax.broadcasted_iota(jnp.int32, (tm, 1), 0)
    for g, w in enumerate(POOL_WINDOWS):
        cols = slice(g * POOL_GROUP_WIDTH, (g + 1) * POOL_GROUP_WIDTH)
        tok = ext_ref[HALO:HALO + tm, cols]
        win = tok
        for back in range(1, w):
            win = win + ext_ref[HALO - back:HALO - back + tm, cols]
        count = jnp.minimum(w, pos + 1).astype(F32)
        pooled = win / count - tok
        mixed = _dot(pooled.astype(BF16), pw_ref[g])
        o_ref[:, cols] = (mixed * ps_ref[:, cols]).astype(o_ref.dtype)


def _pool(u_arr, halo_arr, u_spec, halo_spec, pool_w, pool_scale, grid, tm, zero_first, pos0):
    rows = grid[0] * grid[1] * tm
    body = functools.partial(_pool_body, tm=tm, zero_first=zero_first, pos0=pos0)
    return pl.pallas_call(
        body, grid=grid,
        in_specs=[u_spec, halo_spec,
                  pl.BlockSpec((POOL_GROUPS, POOL_GROUP_WIDTH, POOL_GROUP_WIDTH), lambda b, i: (0, 0, 0)),
                  pl.BlockSpec((1, POOL_WIDTH), lambda b, i: (0, 0))],
        out_specs=pl.BlockSpec((tm, POOL_WIDTH), lambda b, i: (b * grid[1] + i, 0)),
        out_shape=jax.ShapeDtypeStruct((rows, POOL_WIDTH), BF16),
        scratch_shapes=[pltpu.VMEM((HALO + tm, POOL_WIDTH), F32)],
        compiler_params=_params("parallel", "parallel"), name="pool_mix",
    )(u_arr, halo_arr, pool_w, pool_scale.reshape(1, POOL_WIDTH))


def _rows(start, dil, count=ATTN_CTX):
    return pl.ds(start, count, stride=dil) if dil > 1 else pl.ds(start, count)


def _attn_prompt_body(sl_ref, *refs):
    n_in = 5 * N_ATTN_GROUPS
    ins, o_ref, kv_out, scr = refs[:n_in], refs[n_in], refs[n_in + 1:n_in + 1 + 2 * N_ATTN_GROUPS], \
        refs[n_in + 1 + 2 * N_ATTN_GROUPS:]
    acc_ref, m_ref, l_ref = scr[2 * N_ATTN_GROUPS:]
    head = pl.program_id(1)
    t = pl.program_id(2)
    key_col = lax.broadcasted_iota(jnp.int32, (ATTN_CTX, 2 * ATTN_CTX), 1)
    steps = (ATTN_CTX + lax.broadcasted_iota(jnp.int32, (ATTN_CTX, 2 * ATTN_CTX), 0) - key_col).astype(F32)
    in_band = jnp.logical_and(steps >= 0.0, steps <= float(ATTN_CTX))
    ones = jnp.ones((2 * ATTN_CTX, LANES), BF16)
    scale = 1.0 / math.sqrt(HEAD_DIM)
    for g, (window, dil) in reversed(list(enumerate(ATTN_GROUPS))):
        first_group = g == N_ATTN_GROUPS - 1
        q_ref, k_ref, v_ref, kp_ref, vp_ref = ins[5 * g:5 * g + 5]
        kx_ref, vx_ref = scr[2 * g], scr[2 * g + 1]
        fold = ATTN_FOLD if dil % SUBLANES == 0 else 1
        seg = (window + ATTN_TILE) // fold
        for c in range(fold):
            for dst_ref, prev_ref, cur_ref in ((kx_ref, kp_ref, k_ref), (vx_ref, vp_ref, v_ref)):
                dst_ref[c * seg:c * seg + window // fold, :] = prev_ref[_rows(c, fold, window // fold), :]
                dst_ref[c * seg + window // fold:(c + 1) * seg, :] = cur_ref[_rows(c, fold, ATTN_TILE // fold), :]
        kv_out[2 * g][...] = k_ref[ATTN_TILE - window:, :]
        kv_out[2 * g + 1][...] = v_ref[ATTN_TILE - window:, :]
        bias = (sl_ref[g, head] * float(dil)) * steps

        def attend(idx, window=window, dil=dil, q_ref=q_ref, kx_ref=kx_ref, vx_ref=vx_ref, bias=bias,
                   fold=fold, seg=seg):
            n = idx // dil
            r = idx % dil
            start = n * window + r
            q = q_ref[_rows(start, dil), :].astype(BF16)
            keys = _rows((start % fold) * seg + start // fold, dil // fold, 2 * ATTN_CTX)
            k2 = kx_ref[keys, :].astype(BF16)
            v2 = vx_ref[keys, :].astype(BF16)
            first_key = jnp.where(jnp.logical_or(t > 0, n > 0), 0, ATTN_CTX)
            valid = jnp.logical_and(in_band, key_col >= first_key)
            s = jnp.where(valid, _dot_nt(q, k2) * scale - bias, NEG_INF)
            m = jnp.max(s, axis=1, keepdims=True)
            p = jnp.exp(s - m).astype(BF16)
            both = _dot(p, jnp.concatenate([v2, ones], axis=1))
            m = jnp.broadcast_to(m, (ATTN_CTX, LANES))
            return _rows(start, dil), both[:, :HEAD_DIM], m, both[:, HEAD_DIM:]

        def merge(rows, acc, m, l, first_group=first_group):
            if not first_group:
                m_old = m_ref[rows, :]
                m_new = jnp.maximum(m_old, m)
                a_old = jnp.exp(m_old - m_new)
                a_new = jnp.exp(m - m_new)
                acc = acc_ref[rows, :] * a_old + acc * a_new
                l = l_ref[rows, :] * a_old + l * a_new
                m = m_new
            acc_ref[rows, :] = acc
            m_ref[rows, :] = m
            l_ref[rows, :] = l

        def blocks(it, carry, attend=attend, merge=merge):
            done = [attend(it * ATTN_UNROLL + u) for u in range(ATTN_UNROLL)]
            for part in done:
                merge(*part)
            return carry

        lax.fori_loop(0, ATTN_TILE // ATTN_CTX // ATTN_UNROLL, blocks, 0)
    o_ref[...] = (acc_ref[...] / l_ref[...]).astype(o_ref.dtype)


def _attn_prompt(z, slopes, batch, seq):
    tiles = seq // ATTN_TILE
    col0 = QKV_OFF // HEAD_DIM
    in_specs = [pl.BlockSpec(memory_space=pltpu.SMEM)]
    operands = [slopes]
    scratch = []
    for g, (window, _) in enumerate(ATTN_GROUPS):
        per_tile = ATTN_TILE // window
        for part in range(3):
            col = col0 + (g * 3 + part) * HEADS
            in_specs.append(pl.BlockSpec((ATTN_TILE, HEAD_DIM),
                                         lambda b, h, t, col=col: (b * tiles + t, col + h)))
            operands.append(z)
        for part in (1, 2):
            col = col0 + (g * 3 + part) * HEADS
            in_specs.append(pl.BlockSpec(
                (window, HEAD_DIM),
                lambda b, h, t, col=col, per_tile=per_tile:
                (jnp.maximum((b * tiles + t) * per_tile - 1, 0), col + h)))
            operands.append(z)
        scratch += [pltpu.VMEM((window + ATTN_TILE, HEAD_DIM), F32)] * 2
    scratch += [pltpu.VMEM((ATTN_TILE, LANES), F32)] * 3
    out_specs = [pl.BlockSpec((ATTN_TILE, HEAD_DIM), lambda b, h, t: (b * tiles + t, h))]
    out_shape = [jax.ShapeDtypeStruct((batch * seq, ATTN_WIDTH), BF16)]
    for window, _ in ATTN_GROUPS:
        for _ in range(2):
            out_specs.append(pl.BlockSpec((window, HEAD_DIM), lambda b, h, t: (b, h)))
            out_shape.append(jax.ShapeDtypeStruct((batch * window, ATTN_WIDTH), F32))
    outs = pl.pallas_call(
        _attn_prompt_body, grid=(batch, HEADS, tiles),
        in_specs=in_specs, out_specs=out_specs, out_shape=out_shape,
        scratch_shapes=scratch,
        compiler_params=_params("parallel", "parallel", "arbitrary"), name="attn_prompt",
    )(*operands)
    return outs[0], [(outs[1 + 2 * g], outs[2 + 2 * g]) for g in range(N_ATTN_GROUPS)]


def _attn_sample_body(sl_ref, new_ref, c0_ref, c1_ref, c2_ref, o_ref, *, dec_seq):
    caches = (c0_ref, c1_ref, c2_ref)
    scale = 1.0 / math.sqrt(HEAD_DIM)
    ones = jnp.ones((HEAD_DIM, LANES), BF16)
    row = lax.broadcasted_iota(jnp.int32, (ATTN_CTX, HEADS, LANES), 0)

    def lane_sums(x):
        flat = x.reshape(-1, HEAD_DIM)
        hi = flat.astype(BF16)
        lo = (flat - hi.astype(F32)).astype(BF16)
        return (_dot(hi, ones) + _dot(lo, ones)).reshape(x.shape[:-1] + (LANES,))

    state = [None] * dec_seq
    for g, (window, dil) in enumerate(ATTN_GROUPS):
        c_ref = caches[g]
        slope = sl_ref[g]
        bias = slope[None] * ((ATTN_CTX - row) * dil).astype(F32)
        for s in range(dec_seq):
            q = new_ref[s, 3 * g]
            res = s % dil
            extra = s - res
            sc = lane_sums(q[None] * c_ref[:, res, 0]) * scale - bias
            if extra > 0:
                sc = jnp.where(row >= extra // dil, sc - slope[None] * float(extra), NEG_INF)
            fresh = [j for j in range(s + 1) if (s - j) % dil == 0]
            dots = lane_sums(q[None] * jnp.stack([new_ref[j, 3 * g + 1] for j in fresh])) * scale
            sc_new = [dots[i] - slope * float(s - j) for i, j in enumerate(fresh)]
            m = functools.reduce(jnp.maximum, sc_new, jnp.max(sc, axis=0))
            p = jnp.exp(sc - m[None])
            a = jnp.sum(p * c_ref[:, res, 1], axis=0)
            l = jnp.sum(p, axis=0)
            for j, sc_j in zip(fresh, sc_new):
                p_j = jnp.exp(sc_j - m)
                a, l = a + p_j * new_ref[j, 3 * g + 2], l + p_j
            if g > 0:
                acc, m_run, l_run = state[s]
                m_new = jnp.maximum(m_run, m)
                a_old = jnp.exp(m_run - m_new)
                a_cur = jnp.exp(m - m_new)
                a, l, m = acc * a_old + a * a_cur, l_run * a_old + l * a_cur, m_new
            state[s] = (a, m, l)
    for s in range(dec_seq):
        o_ref[s] = state[s][0] / state[s][2]


def _attn_sample(new_rows, caches, slopes, dec_batch, dec_seq):
    tile = (HEADS, HEAD_DIM)
    in_specs = [pl.BlockSpec((N_ATTN_GROUPS,) + tile, lambda n: (0, 0, 0)),
                pl.BlockSpec((None, dec_seq, 3 * N_ATTN_GROUPS) + tile, lambda n: (n, 0, 0, 0, 0))]
    operands = [jnp.broadcast_to(slopes[:, :, None], (N_ATTN_GROUPS,) + tile), new_rows]
    for (window, dil), cache in zip(ATTN_GROUPS, caches):
        n_res = min(dil, dec_seq)
        in_specs.append(pl.BlockSpec((None, window // dil, n_res, 2) + tile, lambda n: (n, 0, 0, 0, 0, 0)))
        operands.append(cache.reshape((dec_batch, window // dil, dil, 2) + tile))
    return pl.pallas_call(
        functools.partial(_attn_sample_body, dec_seq=dec_seq), grid=(dec_batch,),
        in_specs=in_specs,
        out_specs=pl.BlockSpec((None, dec_seq) + tile, lambda n: (n, 0, 0, 0)),
        out_shape=jax.ShapeDtypeStruct((dec_batch, dec_seq) + tile, F32),
        compiler_params=_params("parallel"), name="attn_sample",
    )(*operands)


def _merge_body(p_ref, a_ref, wp_ref, wa_ref, gp_ref, ga_ref, o_ref):
    yp = _dot(p_ref[...], wp_ref[...])
    ya = _dot(a_ref[...], wa_ref[...])
    o_ref[...] = (jax.nn.sigmoid(gp_ref[...]) * yp + jax.nn.sigmoid(ga_ref[...]) * ya).astype(o_ref.dtype)


def _merge(pool_out, attn_out, w_pool_proj, w_attn_proj, z, tm, tn):
    rows = pool_out.shape[0]
    gp0 = GATE_OFF // tn
    ga0 = (GATE_OFF + D_MODEL) // tn
    return pl.pallas_call(
        _merge_body, grid=(rows // tm, D_MODEL // tn),
        in_specs=[pl.BlockSpec((tm, POOL_WIDTH), lambda i, j: (i, 0)),
                  pl.BlockSpec((tm, ATTN_WIDTH), lambda i, j: (i, 0)),
                  pl.BlockSpec((POOL_WIDTH, tn), lambda i, j: (0, j)),
                  pl.BlockSpec((ATTN_WIDTH, tn), lambda i, j: (0, j)),
                  pl.BlockSpec((tm, tn), lambda i, j: (i, gp0 + j)),
                  pl.BlockSpec((tm, tn), lambda i, j: (i, ga0 + j))],
        out_specs=pl.BlockSpec((tm, tn), lambda i, j: (i, j)),
        out_shape=jax.ShapeDtypeStruct((rows, D_MODEL), BF16),
        compiler_params=_params("parallel", "parallel"), name="gated_merge",
    )(pool_out, attn_out, w_pool_proj, w_attn_proj, z, z)


def _resid_body(m_ref, w_ref, x_ref, ga_ref, o_ref):
    o_ref[...] = x_ref[...] + ga_ref[...] * _dot(m_ref[...], w_ref[...])


def _resid_proj(merged, w_out, x, mod, gate_chunk, tm, tn):
    rows = merged.shape[0]
    return pl.pallas_call(
        _resid_body, grid=(rows // tm, D_MODEL // tn),
        in_specs=[pl.BlockSpec((tm, D_MODEL), lambda i, j: (i, 0)),
                  pl.BlockSpec((D_MODEL, tn), lambda i, j: (0, j)),
                  pl.BlockSpec((tm, tn), lambda i, j: (i, j)),
                  mod.spec(gate_chunk, tm, tn, ncols=2)],
        out_specs=pl.BlockSpec((tm, tn), lambda i, j: (i, j)),
        out_shape=jax.ShapeDtypeStruct((rows, D_MODEL), F32),
        compiler_params=_params("parallel", "parallel"), name="resid_proj",
    )(merged, w_out, x, mod.arr)


def _odd_even_merge_sort(n):
    pairs = []
    p = 1
    while p < n:
        k = p
        while k >= 1:
            for j in range(k % p, n - k, 2 * k):
                for i in range(min(k, n - j - k)):
                    if (i + j) // (2 * p) == (i + j + k) // (2 * p):
                        pairs.append((i + j, i + j + k))
            k //= 2
        p *= 2
    return pairs


def _top16(s):
    k = PEER_TOPK
    x = [s[SUBLANES * i:SUBLANES * (i + 1), :] for i in range(PEER_NKEYS // SUBLANES)]
    for i, j in _odd_even_merge_sort(k):
        x[i], x[j] = jnp.maximum(x[i], x[j]), jnp.minimum(x[i], x[j])
    dropped = jnp.full_like(x[0], NEG_INF)
    shift = SUBLANES // 2
    while shift >= 1:
        y = [pltpu.roll(v, shift, axis=0) for v in x]
        dropped = jnp.maximum(dropped, pltpu.roll(dropped, shift, axis=0))
        lo = [jnp.minimum(x[i], y[k - 1 - i]) for i in range(k)]
        dropped = functools.reduce(jnp.maximum, lo, dropped)
        x = [jnp.maximum(x[i], y[k - 1 - i]) for i in range(k)]
        stride = k // 2
        while stride >= 1:
            for i in range(k):
                if (i // stride) % 2 == 0:
                    j = i + stride
                    x[i], x[j] = jnp.maximum(x[i], x[j]), jnp.minimum(x[i], x[j])
            stride //= 2
        shift //= 2
    return x, dropped


def _count_prefix(pred, vals):
    c8 = pred(vals[7])
    c4 = pred(jnp.where(c8, vals[11], vals[3]))
    c2 = pred(jnp.where(c8, jnp.where(c4, vals[13], vals[9]), jnp.where(c4, vals[5], vals[1])))
    c1 = pred(jnp.where(c8, jnp.where(c4, jnp.where(c2, vals[14], vals[12]), jnp.where(c2, vals[10], vals[8])),
                        jnp.where(c4, jnp.where(c2, vals[6], vals[4]), jnp.where(c2, vals[2], vals[0]))))
    weights = ((c8, 8.0), (c4, 4.0), (c2, 2.0), (c1, 1.0), (pred(vals[15]), 1.0))
    return functools.reduce(jnp.add, [jnp.where(m, w, 0.0) for m, w in weights])


def _gate_rows(ref, h, row):
    r = ref[h, row:row + 1, :]
    return jnp.broadcast_to(r, (2 * SUBLANES, r.shape[1])).astype(BF16)


def _route_body(xT_ref, wq_ref, sk_ref, n_ref, c_ref, rank_ref, e1_ref, q_ref):
    q_ref[...] = _dot(wq_ref[...], xT_ref[...])
    k = PEER_TOPK

    def head(h, carry):
        half = []
        for p in range(2):
            row0 = pl.multiple_of((h * 2 + p) * PEER_NKEYS, PEER_NKEYS)
            q = q_ref[pl.ds(row0, PEER_NKEYS), :].astype(BF16)
            half.append(_dot(sk_ref[h * 2 + p], q))
        s0, s1 = half
        v0, rest0 = _top16(s0)
        v1, rest1 = _top16(s1)
        row = lax.broadcasted_iota(jnp.int32, v0[0].shape, 0)

        def along_sublanes(vals):
            return functools.reduce(lambda acc, b: jnp.where(row == b, vals[b], acc), range(1, SUBLANES), vals[0])

        v1_lo, v1_hi, v0_hi = along_sublanes(v1[:SUBLANES]), along_sublanes(v1[SUBLANES:]), along_sublanes(v0[SUBLANES:])
        cand = [v0[0] + v1_lo, v0[0] + v1_hi, v0_hi + v1[0]]
        for a in range(1, SUBLANES):
            cand.append(jnp.where(row <= (k + 1) // (a + 1) - 1, v0[a] + v1_lo, NEG_INF))
        top = (v0[0] + v1[0])[0:1, :]
        taken = jnp.zeros_like(top)
        tau = jnp.full_like(top, NEG_INF)
        below = jnp.full_like(top, NEG_INF)
        z = jnp.zeros_like(top)
        for _ in range(k + 1):
            mx = jnp.max(functools.reduce(jnp.maximum, cand), axis=0, keepdims=True)
            eqs = [ca == mx for ca in cand]
            cnt = jnp.sum(functools.reduce(jnp.add, [jnp.where(eq, 1.0, 0.0) for eq in eqs]), axis=0, keepdims=True)
            cand = [jnp.where(eq, NEG_INF, ca) for eq, ca in zip(eqs, cand)]
            need = taken < float(k)
            first_below = jnp.logical_and(jnp.logical_not(need), below <= 0.5 * NEG_INF)
            tau = jnp.where(need, mx, tau)
            z = z + jnp.where(need, cnt * jnp.exp(mx - top), 0.0)
            below = jnp.where(first_below, mx, below)
            taken = taken + jnp.where(need, cnt, 0.0)
        below = jnp.maximum(below, jnp.maximum(rest0 + v1[0], v0[0] + rest1)[0:1, :])
        cut = 0.5 * (tau + below)
        v1_rows = [v[0:1, :] for v in v1]
        rank1 = _count_prefix(lambda v: s1 < v, v1_rows)
        n_pass = _count_prefix(lambda v: s0 + v >= cut, v1_rows)
        n_ref[h] = n_pass
        c_ref[h] = jnp.exp(s0 - v0[0][0:1, :]) / z
        rank_ref[h] = rank1.astype(rank_ref.dtype)
        e1_ref[h] = jnp.exp(s1 - v1[0][0:1, :]).astype(e1_ref.dtype)
        return carry

    lax.fori_loop(0, PEER_HEADS, head, 0)


def _route(xT, w_pqT, sub_keys16, tm):
    d, tokens = xT.shape
    nq = w_pqT.shape[0]
    shape = (PEER_HEADS, PEER_NKEYS, tokens)
    ospec = pl.BlockSpec((PEER_HEADS, PEER_NKEYS, tm), lambda i: (0, 0, i))
    return pl.pallas_call(
        _route_body, grid=(tokens // tm,),
        in_specs=[pl.BlockSpec((d, tm), lambda i: (0, i)),
                  pl.BlockSpec((nq, d), lambda i: (0, 0)),
                  pl.BlockSpec((2 * PEER_HEADS, PEER_NKEYS, sub_keys16.shape[2]), lambda i: (0, 0, 0))],
        out_specs=[ospec] * 4,
        out_shape=[jax.ShapeDtypeStruct(shape, F32), jax.ShapeDtypeStruct(shape, F32),
                   jax.ShapeDtypeStruct(shape, BF16), jax.ShapeDtypeStruct(shape, BF16)],
        scratch_shapes=[pltpu.VMEM((nq, tm), F32)],
        compiler_params=_params("parallel"), name="peer_route",
    )(xT, w_pqT, sub_keys16)


def _peer_body(xT_ref, u_ref, vT_ref, na_ref, ca_ref, ranka_ref, e1a_ref, nb_ref, cb_ref, rankb_ref, e1b_ref,
               o_ref, act_ref, gated_ref, *, steps):
    g = pl.program_id(0)
    eb = EXPERT_BLOCK_I * PEER_NKEYS

    @pl.when(g == 0)
    def _():
        act_ref[...] = jnp.zeros_like(act_ref)
        gated_ref[...] = jnp.zeros_like(gated_ref)

    @pl.when(jnp.logical_or(g == 0, g % steps == 1))
    def _():
        o_ref[...] = jnp.zeros_like(o_ref)

    def stage_a(rows):
        return jax.nn.gelu(_dot(u_ref[rows, :], xT_ref[...])).astype(BF16)

    chunk = 16
    n_chunks = PEER_NKEYS // chunk

    def stage_b(act, n_ref, c_ref, rank_ref, e1_ref, row0):
        parts = []
        for ii in range(EXPERT_BLOCK_I):
            w = [None] * n_chunks
            for h in range(PEER_HEADS):
                n_rows = _gate_rows(n_ref, h, row0 + ii)
                c_rows = _gate_rows(c_ref, h, row0 + ii)
                for jc in range(n_chunks):
                    keys = slice(jc * chunk, (jc + 1) * chunk)
                    sel = jnp.where(rank_ref[h, keys, :] < n_rows, e1_ref[h, keys, :], 0.0) * c_rows
                    w[jc] = sel if w[jc] is None else w[jc] + sel
            base = ii * PEER_NKEYS
            parts += [act[base + jc * chunk:base + (jc + 1) * chunk, :] * w[jc] for jc in range(n_chunks)]
        return jnp.concatenate(parts, axis=0)

    read, write = (g + 1) % 2, g % 2
    act0 = stage_a(slice(0, eb))
    gated1 = stage_b(act_ref[read], na_ref, ca_ref, ranka_ref, e1a_ref, EXPERT_BLOCK_I)
    act1 = stage_a(slice(eb, 2 * eb))
    gated0 = stage_b(act0, nb_ref, cb_ref, rankb_ref, e1b_ref, 0)
    o_ref[...] += _dot(vT_ref[...], jnp.concatenate([gated_ref[read], gated1], axis=0))
    act_ref[write] = act1
    gated_ref[write] = gated0


def _peer_mix(xT, peer_u, peer_vT, n_pass, c, rank1, e1, tm):
    d, tokens = xT.shape
    eb = EXPERT_BLOCK_I * PEER_NKEYS
    steps = peer_u.shape[0] // (2 * eb)
    tiles = tokens // tm

    def cur(g):
        return g % steps, jnp.minimum(g // steps, tiles - 1)

    def lag(g):
        return cur(jnp.maximum(g - 1, 0))

    def rows_spec(at):
        return pl.BlockSpec((PEER_HEADS, 2 * EXPERT_BLOCK_I, tm), lambda g: (0,) + at(g))

    def cols_spec(at):
        return pl.BlockSpec((PEER_HEADS, PEER_NKEYS, tm), lambda g: (0, 0, at(g)[1]))

    return pl.pallas_call(
        functools.partial(_peer_body, steps=steps), grid=(tiles * steps + 1,),
        in_specs=[pl.BlockSpec((d, tm), lambda g: (0, cur(g)[1])),
                  pl.BlockSpec((2 * eb, d), lambda g: (cur(g)[0], 0)),
                  pl.BlockSpec((d, 2 * eb), lambda g: (0, lag(g)[0])),
                  rows_spec(lag), rows_spec(lag), cols_spec(lag), cols_spec(lag),
                  rows_spec(cur), rows_spec(cur), cols_spec(cur), cols_spec(cur)],
        out_specs=pl.BlockSpec((d, tm), lambda g: (0, lag(g)[1])),
        out_shape=jax.ShapeDtypeStruct((d, tokens), F32),
        scratch_shapes=[pltpu.VMEM((2, eb, tm), BF16), pltpu.VMEM((2, eb, tm), BF16)],
        compiler_params=_params("arbitrary"), name="peer_mix",
    )(xT, peer_u, peer_vT, n_pass, c, rank1, e1, n_pass, c, rank1, e1)


def _final_body(h_ref, pT_ref, ga_ref, g_ref, o_ref):
    h = h_ref[...] + ga_ref[...] * pT_ref[...].T
    o_ref[...] = h * lax.rsqrt(jnp.mean(h * h, axis=-1, keepdims=True) + NORM_EPS) * g_ref[...]


def _final(h, peerT, mod, gate_chunk, g_final, tm):
    rows, d = h.shape
    return pl.pallas_call(
        _final_body, grid=(rows // tm,),
        in_specs=[pl.BlockSpec((tm, d), lambda i: (i, 0)),
                  pl.BlockSpec((d, tm), lambda i: (0, i)),
                  mod.spec(gate_chunk, tm),
                  pl.BlockSpec((1, d), lambda i: (0, 0))],
        out_specs=pl.BlockSpec((tm, d), lambda i: (i, 0)),
        out_shape=jax.ShapeDtypeStruct((rows, d), F32),
        compiler_params=_params("parallel"), name="final_norm",
    )(h, peerT, mod.arr, g_final.reshape(1, d))


def _alibi_slopes():
    idx = jnp.arange(N_ATTN_GROUPS * HEADS, dtype=F32) + 1.0
    return jnp.exp2(-8.0 * idx / (N_ATTN_GROUPS * HEADS)).reshape(N_ATTN_GROUPS, HEADS)


def _peer(xT, mod, h, wts, g_final, tm_route, tm_mix, tm_final):
    n_pass, c, rank1, e1 = _route(xT, wts["w_pqT"], wts["sub_keys"], tm_route)
    peerT = _peer_mix(xT, wts["peer_u"], wts["peer_vT"], n_pass, c, rank1, e1, tm_mix)
    return _final(h, peerT, mod, 5, g_final, tm_final)


def kernel(x_prompt, x_sample, state_pool, cache_kv0, cache_kv1, cache_kv2, c_prompt, c_sample, w_ada, b_ada,
           g_norm1, g_norm2, w_in, pool_w, pool_scale, w_pool_proj, w_attn_proj, w_out, w_pq, sub_keys, peer_u,
           peer_v, g_final):
    batch, seq, d = x_prompt.shape
    dec_batch, dec_seq, _ = x_sample.shape
    depth = w_ada.shape[0]
    assert depth == 1 and d == D_MODEL and seq % ATTN_TILE == 0 and dec_seq <= 8
    caches = (cache_kv0[0], cache_kv1[0], cache_kv2[0])
    for (window, _), cache in zip(ATTN_GROUPS, caches):
        assert cache.shape[1] == window

    wts = {
        "w_in": w_in[0].astype(BF16),
        "pool_w": pool_w[0].astype(BF16),
        "w_pool_proj": w_pool_proj[0].astype(BF16),
        "w_attn_proj": w_attn_proj[0].astype(BF16),
        "w_out": w_out[0].astype(BF16),
        "w_pqT": w_pq[0].T.astype(BF16),
        "sub_keys": sub_keys[0].reshape(2 * PEER_HEADS, PEER_NKEYS, -1).astype(BF16),
        "peer_u": peer_u[0].astype(BF16),
        "peer_vT": peer_v[0].T.astype(BF16),
    }
    slopes = _alibi_slopes()

    n_cond = batch + dec_batch
    cond_rows = -(-n_cond // 8) * 8
    c_all = jnp.concatenate([c_prompt, c_sample, jnp.zeros((cond_rows - n_cond, d), F32)], axis=0)
    mod_all = _adaln(c_all, w_ada[0], b_ada[0])
    mod_p = _Mod(mod_all[:batch].reshape(batch, 1, 6 * d), seq)
    s_rows = dec_batch * dec_seq
    mod_s = _Mod(jnp.repeat(mod_all[batch:n_cond], dec_seq, axis=0), None)

    xp = x_prompt.reshape(batch * seq, d)
    n1 = _norm_mod(xp, g_norm1[0], mod_p, 0, 1, 1024)
    z = _matmul(n1, wts["w_in"], 2048, 1024)
    tm_pool = 512
    nt = seq // tm_pool
    pool_out = _pool(
        z, z,
        pl.BlockSpec((tm_pool, POOL_WIDTH), lambda b, i: (b * nt + i, 0)),
        pl.BlockSpec((HALO, POOL_WIDTH), lambda b, i: (jnp.maximum((b * nt + i) * (tm_pool // HALO) - 1, 0), 0)),
        wts["pool_w"], pool_scale[0], (batch, nt), tm_pool, True, 0)
    attn_out, kv_rows = _attn_prompt(z, slopes, batch, seq)
    merged = _merge(pool_out, attn_out, wts["w_pool_proj"], wts["w_attn_proj"], z, 512, D_MODEL)
    hp = _resid_proj(merged, wts["w_out"], xp, mod_p, 2, 512, D_MODEL)
    n2T = _norm_mod(hp, g_norm2[0], mod_p, 3, 4, 1024, transpose=True)
    y_prompt = _peer(n2T, mod_p, hp, wts, g_final, 256, 512, 512).reshape(batch, seq, d)

    new_pool_p = z.reshape(batch, seq, PROJ_WIDTH)[:, seq - POOL_HIST:, :POOL_WIDTH][None]
    new_kv_p = [jnp.stack([k.reshape(batch, -1, HEADS, HEAD_DIM), v.reshape(batch, -1, HEADS, HEAD_DIM)], axis=2)[None]
                for k, v in kv_rows]

    xs =x_sample.reshape(s_rows, d)
    n1s = _norm_mod(xs, g_norm1[0], mod_s, 0, 1, s_rows)
    zs = _matmul(n1s, wts["w_in"], s_rows, 1024)
    zs3 = zs.reshape(dec_batch, dec_seq, PROJ_WIDTH)
    pad_rows = 16
    zs_pad = jnp.pad(zs3, ((0, 0), (0, pad_rows - dec_seq), (0, 0)))
    u_pad = zs_pad[:, :, :POOL_WIDTH].reshape(dec_batch * pad_rows, POOL_WIDTH)
    halo_s = jnp.pad(state_pool[0], ((0, 0), (HALO - POOL_HIST, 0), (0, 0))).reshape(dec_batch * HALO, POOL_WIDTH)
    pool_s = _pool(
        u_pad, halo_s,
        pl.BlockSpec((pad_rows, POOL_WIDTH), lambda b, i: (b, 0)),
        pl.BlockSpec((HALO, POOL_WIDTH), lambda b, i: (b, 0)),
        wts["pool_w"], pool_scale[0], (dec_batch, 1), pad_rows, False, PAST_LEN)
    pool_s = pool_s.reshape(dec_batch, pad_rows, POOL_WIDTH)[:, :dec_seq].reshape(s_rows, POOL_WIDTH)
    new_rows = zs3[:, :, QKV_OFF:GATE_OFF].reshape(dec_batch, dec_seq, 3 * N_ATTN_GROUPS, HEADS, HEAD_DIM)
    attn_s = _attn_sample(new_rows, caches, slopes, dec_batch, dec_seq).reshape(s_rows, ATTN_WIDTH).astype(BF16)
    merged_s = _merge(pool_s, attn_s, wts["w_pool_proj"], wts["w_attn_proj"], zs, s_rows, 1024)
    hs = _resid_proj(merged_s, wts["w_out"], xs, mod_s, 2, s_rows, 1024)
    n2s = _norm_mod(hs, g_norm2[0], mod_s, 3, 4, s_rows)
    lane_rows = -(-s_rows // LANES) * LANES
    n2s_pad = jnp.pad(n2s, ((0, lane_rows - s_rows), (0, 0)))
    hs_pad = jnp.pad(hs, ((0, lane_rows - s_rows), (0, 0)))
    mod_s_pad = _Mod(jnp.pad(mod_s.arr, ((0, lane_rows - s_rows), (0, 0))), None)
    y_sample = _peer(n2s_pad.T, mod_s_pad, hs_pad, wts, g_final, lane_rows, lane_rows, lane_rows)
    y_sample = y_sample[:s_rows].reshape(dec_batch, dec_seq, d)

    new_pool_s = jnp.concatenate([state_pool[0], zs3[:, :, :POOL_WIDTH]], axis=1)[:, -POOL_HIST:][None]
    new_kv_s = []
    for g, ((window, _), cache) in enumerate(zip(ATTN_GROUPS, caches)):
        off = QKV_OFF + (g * 3 + 1) * ATTN_WIDTH
        new_rows = zs3[:, :, off:off + 2 * ATTN_WIDTH].reshape(dec_batch, dec_seq, 2, HEADS, HEAD_DIM)
        new_kv_s.append(jnp.concatenate([cache, new_rows], axis=1)[:, -window:][None])

    return (y_prompt, y_sample, new_pool_p, new_kv_p[0], new_kv_p[1], new_kv_p[2],
            new_pool_s, new_kv_s[0], new_kv_s[1], new_kv_s[2])
```
